```python
import math
import jax
import jax.numpy as jnp
from jax import lax
import numpy as np

D_MODEL = 2048
BATCH = 16
SEQ = 2048
DEPTH = 4
DEC_BATCH = 2
DEC_SEQ = 8192
PAST_LEN = 128

HEAD_DIM = 128
N_Q_HEADS = (D_MODEL // 2) // HEAD_DIM
N_KV_HEADS = 2
Q_PER_KV = N_Q_HEADS // N_KV_HEADS
ATTN_W = N_Q_HEADS * HEAD_DIM
KV_W = N_KV_HEADS * HEAD_DIM
WINDOW = 128
BLOCK = 128
ROPE_DIM = HEAD_DIM // 4
ROPE_THETA = 500000.0
SSM_W = D_MODEL // 4
SSM_H = 16
SSM_G = SSM_W // SSM_H
SSM_P = 64
DT_MIN = 1e-3
DT_MAX = 1e-1
CONV_W = D_MODEL // 4
CONV_K = 31
D_MIX = ATTN_W + SSM_W + CONV_W
IN_COLS = ATTN_W + 2 * KV_W + SSM_W + 2 * CONV_W
MEM_LEN = 256
X_HEADS = 4
X_HEAD_DIM = 128
X_W = X_HEADS * X_HEAD_DIM
D_FF = 5632
EPS = 1e-6
NEG = -1e30

kernel_name = 'hymba_style_s5_swa_conformer_encoder'


def rmsnorm(x, g):
    xf = x.astype(jnp.float32)
    y = xf * lax.rsqrt(jnp.mean(xf * xf, axis=-1, keepdims=True) + EPS)
    return (y * g.astype(jnp.float32)).astype(x.dtype)


def layernorm(x, g, b):
    xf = x.astype(jnp.float32)
    mu = jnp.mean(xf, axis=-1, keepdims=True)
    var = jnp.mean(jnp.square(xf - mu), axis=-1, keepdims=True)
    y = (xf - mu) * lax.rsqrt(var + EPS)
    return (y * g.astype(jnp.float32) + b.astype(jnp.float32)).astype(x.dtype)


def swiglu(x, w_gate, w_up, w_down):
    return (jax.nn.silu(x @ w_gate) * (x @ w_up)) @ w_down


def rope_partial(x, pos):
    half = ROPE_DIM // 2
    inv_freq = ROPE_THETA ** (-(2.0 * jnp.arange(half, dtype=jnp.float32)) / ROPE_DIM)
    ang = pos.astype(jnp.float32)[:, None] * inv_freq[None, :]
    cos = jnp.cos(ang)[None, :, None, :]
    sin = jnp.sin(ang)[None, :, None, :]
    xr = x[..., :ROPE_DIM].astype(jnp.float32)
    x1, x2 = xr[..., :half], xr[..., half:]
    rot = jnp.concatenate([x1 * cos - x2 * sin, x2 * cos + x1 * sin], axis=-1).astype(x.dtype)
    return jnp.concatenate([rot, x[..., ROPE_DIM:]], axis=-1)


def window_gqa_with_sink(q, k, v, sink):
    B, L = q.shape[0], q.shape[1]
    nb = L // BLOCK
    qb = q.reshape(B, nb, BLOCK, N_KV_HEADS, Q_PER_KV, HEAD_DIM)
    pad = ((0, 0), (BLOCK, BLOCK), (0, 0), (0, 0))
    kp = jnp.pad(k, pad).reshape(B, nb + 2, BLOCK, N_KV_HEADS, HEAD_DIM)
    vp = jnp.pad(v, pad).reshape(B, nb + 2, BLOCK, N_KV_HEADS, HEAD_DIM)
    kb = jnp.concatenate([kp[:, :-2], kp[:, 1:-1], kp[:, 2:]], axis=2)
    vb = jnp.concatenate([vp[:, :-2], vp[:, 1:-1], vp[:, 2:]], axis=2)
    s = jnp.einsum('bnqhgd,bnchd->bnhgqc', qb, kb, preferred_element_type=jnp.float32)
    s = s * (1.0 / math.sqrt(HEAD_DIM))
    blk = jnp.arange(nb, dtype=jnp.int32)[:, None]
    qpos = blk * BLOCK + jnp.arange(BLOCK, dtype=jnp.int32)[None, :]
    kpos = (blk - 1) * BLOCK + jnp.arange(3 * BLOCK, dtype=jnp.int32)[None, :]
    rel = kpos[:, None, :] - qpos[:, :, None]
    valid = (jnp.abs(rel) <= WINDOW) & (kpos[:, None, :] >= 0) & (kpos[:, None, :] < L)
    s = jnp.where(valid[None, :, None, None, :, :], s, NEG)
    sk = sink.astype(jnp.float32).reshape(N_KV_HEADS, Q_PER_KV)[None, None, :, :, None, None]
    m = jnp.maximum(jnp.max(s, axis=-1, keepdims=True), sk)
    p = jnp.exp(s - m)
    denom = jnp.sum(p, axis=-1, keepdims=True) + jnp.exp(sk - m)
    o = jnp.einsum('bnhgqc,bnchd->bnqhgd', (p / denom).astype(v.dtype), vb)
    return o.reshape(B, L, ATTN_W)


def _linear_recurrence(e_prev, e_next):
    a1r, a1i, b1r, b1i = e_prev
    a2r, a2i, b2r, b2i = e_next
    ar = a2r * a1r - a2i * a1i
    ai = a2r * a1i + a2i * a1r
    br = a2r * b1r - a2i * b1i + b2r
    bi = a2r * b1i + a2i * b1r + b2i
    return (ar, ai, br, bi)


def s5_bidirectional(u, lam_re, lam_im, log_dt, b_re, b_im, c_re, c_im, d_skip, w_glu, b_glu):
    B, L = u.shape[0], u.shape[1]
    uf = u.astype(jnp.float32).reshape(B, L, SSM_G, SSM_H)
    lr = jnp.minimum(lam_re.astype(jnp.float32), -1e-4)
    li = lam_im.astype(jnp.float32)
    dt = jnp.exp(log_dt.astype(jnp.float32))[..., None]
    mag = jnp.exp(lr * dt)
    ab_re = mag * jnp.cos(li * dt)
    ab_im = mag * jnp.sin(li * dt)
    den = lr * lr + li * li
    n_re = ab_re - 1.0
    n_im = ab_im
    f_re = (n_re * lr + n_im * li) / den
    f_im = (n_im * lr - n_re * li) / den
    br = b_re.astype(jnp.float32)
    bi = b_im.astype(jnp.float32)
    bb_re = f_re[..., None] * br - f_im[..., None] * bi
    bb_im = f_re[..., None] * bi + f_im[..., None] * br
    y = d_skip.astype(jnp.float32).reshape(SSM_G, SSM_H) * uf
    for d in (0, 1):
        bu_re = jnp.einsum('blgh,gph->blgp', uf, bb_re[d])
        bu_im = jnp.einsum('blgh,gph->blgp', uf, bb_im[d])
        a_re = jnp.broadcast_to(ab_re[d], bu_re.shape)
        a_im = jnp.broadcast_to(ab_im[d], bu_re.shape)
        _, _, s_re, s_im = lax.associative_scan(
            _linear_recurrence, (a_re, a_im, bu_re, bu_im), reverse=(d == 1), axis=1)
        y = y + jnp.einsum('blgp,ghp->blgh', s_re, c_re[d].astype(jnp.float32)) \
              - jnp.einsum('blgp,ghp->blgh', s_im, c_im[d].astype(jnp.float32))
    y = y.reshape(B, L, SSM_W).astype(u.dtype)
    g = jax.nn.gelu(y)
    return g * jax.nn.sigmoid(g @ w_glu + b_glu)


def conformer_conv(c, conv_w, conv_b, ln_g, ln_b):
    val, gate = c[..., :CONV_W], c[..., CONV_W:]
    g = val * jax.nn.sigmoid(gate)
    w = conv_w.astype(g.dtype)[:, None, :]
    g = lax.conv_general_dilated(
        g, w, window_strides=(1,), padding=[(CONV_K // 2, CONV_K // 2)],
        dimension_numbers=('NWC', 'WIO', 'NWC'), feature_group_count=CONV_W) + conv_b
    return jax.nn.silu(layernorm(g, ln_g, ln_b))


def memory_cross_attention(h, mem_n, w_q, w_kv, w_o):
    B, L = h.shape[0], h.shape[1]
    M = mem_n.shape[1]
    q = (h @ w_q).reshape(B, L, X_HEADS, X_HEAD_DIM)
    kv = (mem_n @ w_kv).reshape(B, M, 2, X_HEADS, X_HEAD_DIM)
    k, v = kv[:, :, 0], kv[:, :, 1]
    s = jnp.einsum('blhd,bmhd->bhlm', q, k, preferred_element_type=jnp.float32)
    p = jax.nn.softmax(s * (1.0 / math.sqrt(X_HEAD_DIM)), axis=-1).astype(v.dtype)
    o = jnp.einsum('bhlm,bmhd->blhd', p, v).reshape(B, L, X_W)
    return o @ w_o


def encoder_layer(x, mem, p):
    B, L = x.shape[0], x.shape[1]
    pos = jnp.arange(L, dtype=jnp.int32)
    x = x + 0.5 * swiglu(rmsnorm(x, p['ffn1_norm']), p['ffn1_w_gate'], p['ffn1_w_up'], p['ffn1_w_down'])
    h = rmsnorm(x, p['mix_norm'])
    z = h @ p['w_in']
    o1 = ATTN_W
    o2 = o1 + KV_W
    o3 = o2 + KV_W
    o4 = o3 + SSM_W
    q = rope_partial(z[..., :o1].reshape(B, L, N_Q_HEADS, HEAD_DIM), pos)
    k = rope_partial(z[..., o1:o2].reshape(B, L, N_KV_HEADS, HEAD_DIM), pos)
    v = z[..., o2:o3].reshape(B, L, N_KV_HEADS, HEAD_DIM)
    u = z[..., o3:o4]
    c = z[..., o4:]
    a_out = window_gqa_with_sink(q, k, v, p['attn_sink'])
    s_out = s5_bidirectional(u, p['ssm_lambda_re'], p['ssm_lambda_im'], p['ssm_log_dt'],
                             p['ssm_b_re'], p['ssm_b_im'], p['ssm_c_re'], p['ssm_c_im'],
                             p['ssm_d'], p['ssm_w_glu'], p['ssm_b_glu'])
    c_out = conformer_conv(c, p['conv_w'], p['conv_b'], p['conv_ln_g'], p['conv_ln_b'])
    mixed = jnp.concatenate([rmsnorm(a_out, p['attn_out_norm']),
                             rmsnorm(s_out, p['ssm_out_norm']),
                             rmsnorm(c_out, p['conv_out_norm'])], axis=-1)
    x = x + mixed @ p['w_out']
    x = x + memory_cross_attention(rmsnorm(x, p['xattn_norm']), rmsnorm(mem, p['mem_norm']),
                                   p['xattn_w_q'], p['xattn_w_kv'], p['xattn_w_o'])
    x = x + 0.5 * swiglu(rmsnorm(x, p['ffn2_norm']), p['ffn2_w_gate'], p['ffn2_w_up'], p['ffn2_w_down'])
    return x


def setup_inputs(seed: int = 0) -> dict:
    key = jax.random.key(seed)
    ks = list(jax.random.split(key, 48))
    counter = [0]
    f32 = jnp.float32

    def nk():
        counter[0] += 1
        return ks[counter[0] - 1]

    def normal(shape, scale):
        return scale * jax.random.normal(nk(), shape, f32)

    def gain(shape):
        return 1.0 + normal(shape, 0.02)

    inp = {}
    inp['x_prompt'] = normal((BATCH, SEQ, D_MODEL), 1.0)
    inp['x_sample'] = normal((DEC_BATCH, DEC_SEQ, D_MODEL), 1.0)
    inp['mem_prompt'] = normal((BATCH, MEM_LEN, D_MODEL), 1.0)
    inp['mem_sample'] = normal((DEC_BATCH, MEM_LEN, D_MODEL), 1.0)
    inp['ffn1_norm'] = gain((DEPTH, D_MODEL))
    inp['ffn1_w_gate'] = normal((DEPTH, D_MODEL, D_FF), D_MODEL ** -0.5)
    inp['ffn1_w_up'] = normal((DEPTH, D_MODEL, D_FF), D_MODEL ** -0.5)
    inp['ffn1_w_down'] = normal((DEPTH, D_FF, D_MODEL), D_FF ** -0.5)
    inp['mix_norm'] = gain((DEPTH, D_MODEL))
    inp['w_in'] = normal((DEPTH, D_MODEL, IN_COLS), D_MODEL ** -0.5)
    inp['attn_sink'] = normal((DEPTH, N_Q_HEADS), 0.5)
    inp['ssm_lambda_re'] = -0.5 + normal((DEPTH, 2, SSM_G, SSM_P), 0.01)
    inp['ssm_lambda_im'] = jnp.pi * jnp.arange(SSM_P, dtype=f32) + normal((DEPTH, 2, SSM_G, SSM_P), 0.01)
    inp['ssm_log_dt'] = jax.random.uniform(nk(), (DEPTH, 2, SSM_G), f32, math.log(DT_MIN), math.log(DT_MAX))
    inp['ssm_b_re'] = normal((DEPTH, 2, SSM_G, SSM_P, SSM_H), (2.0 * SSM_H) ** -0.5)
    inp['ssm_b_im'] = normal((DEPTH, 2, SSM_G, SSM_P, SSM_H), (2.0 * SSM_H) ** -0.5)
    inp['ssm_c_re'] = normal((DEPTH, 2, SSM_G, SSM_H, SSM_P), (2.0 * SSM_P) ** -0.5)
    inp['ssm_c_im'] = normal((DEPTH, 2, SSM_G, SSM_H, SSM_P), (2.0 * SSM_P) ** -0.5)
    inp['ssm_d'] = normal((DEPTH, SSM_W), 0.5)
    inp['ssm_w_glu'] = normal((DEPTH, SSM_W, SSM_W), SSM_W ** -0.5)
    inp['ssm_b_glu'] = normal((DEPTH, SSM_W), 0.01)
    inp['conv_w'] = normal((DEPTH, CONV_K, CONV_W), CONV_K ** -0.5)
    inp['conv_b'] = normal((DEPTH, CONV_W), 0.01)
    inp['conv_ln_g'] = gain((DEPTH, CONV_W))
    inp['conv_ln_b'] = normal((DEPTH, CONV_W), 0.01)
    inp['attn_out_norm'] = gain((DEPTH, ATTN_W))
    inp['ssm_out_norm'] = gain((DEPTH, SSM_W))
    inp['conv_out_norm'] = gain((DEPTH, CONV_W))
    inp['w_out'] = normal((DEPTH, D_MIX, D_MODEL), D_MIX ** -0.5)
    inp['xattn_norm'] = gain((DEPTH, D_MODEL))
    inp['mem_norm'] = gain((DEPTH, D_MODEL))
    inp['xattn_w_q'] = normal((DEPTH, D_MODEL, X_W), D_MODEL ** -0.5)
    inp['xattn_w_kv'] = normal((DEPTH, D_MODEL, 2 * X_W), D_MODEL ** -0.5)
    inp['xattn_w_o'] = normal((DEPTH, X_W, D_MODEL), X_W ** -0.5)
    inp['ffn2_norm'] = gain((DEPTH, D_MODEL))
    inp['ffn2_w_gate'] = normal((DEPTH, D_MODEL, D_FF), D_MODEL ** -0.5)
    inp['ffn2_w_up'] = normal((DEPTH, D_MODEL, D_FF), D_MODEL ** -0.5)
    inp['ffn2_w_down'] = normal((DEPTH, D_FF, D_MODEL), D_FF ** -0.5)
    inp['final_norm'] = gain((D_MODEL,))
    return inp


def reference(x_prompt, x_sample, mem_prompt, mem_sample,
              ffn1_norm, ffn1_w_gate, ffn1_w_up, ffn1_w_down,
              mix_norm, w_in, attn_sink,
              ssm_lambda_re, ssm_lambda_im, ssm_log_dt, ssm_b_re, ssm_b_im, ssm_c_re, ssm_c_im,
              ssm_d, ssm_w_glu, ssm_b_glu,
              conv_w, conv_b, conv_ln_g, conv_ln_b,
              attn_out_norm, ssm_out_norm, conv_out_norm, w_out,
              xattn_norm, mem_norm, xattn_w_q, xattn_w_kv, xattn_w_o,
              ffn2_norm, ffn2_w_gate, ffn2_w_up, ffn2_w_down,
              final_norm):
    stacked = {
        'ffn1_norm': ffn1_norm, 'ffn1_w_gate': ffn1_w_gate, 'ffn1_w_up': ffn1_w_up, 'ffn1_w_down': ffn1_w_down,
        'mix_norm': mix_norm, 'w_in': w_in, 'attn_sink': attn_sink,
        'ssm_lambda_re': ssm_lambda_re, 'ssm_lambda_im': ssm_lambda_im, 'ssm_log_dt': ssm_log_dt,
        'ssm_b_re': ssm_b_re, 'ssm_b_im': ssm_b_im, 'ssm_c_re': ssm_c_re, 'ssm_c_im': ssm_c_im,
        'ssm_d': ssm_d, 'ssm_w_glu': ssm_w_glu, 'ssm_b_glu': ssm_b_glu,
        'conv_w': conv_w, 'conv_b': conv_b, 'conv_ln_g': conv_ln_g, 'conv_ln_b': conv_ln_b,
        'attn_out_norm': attn_out_norm, 'ssm_out_norm': ssm_out_norm, 'conv_out_norm': conv_out_norm,
        'w_out': w_out,
        'xattn_norm': xattn_norm, 'mem_norm': mem_norm,
        'xattn_w_q': xattn_w_q, 'xattn_w_kv': xattn_w_kv, 'xattn_w_o': xattn_w_o,
        'ffn2_norm': ffn2_norm, 'ffn2_w_gate': ffn2_w_gate, 'ffn2_w_up': ffn2_w_up, 'ffn2_w_down': ffn2_w_down,
    }
    xp = x_prompt
    xs = x_sample
    for l in range(DEPTH):
        p = {name: arr[l] for name, arr in stacked.items()}
        xp = encoder_layer(xp, mem_prompt, p)
        xs = encoder_layer(xs, mem_sample, p)
    y_prompt = rmsnorm(xp, final_norm)
    y_sample = rmsnorm(xs, final_norm)
    return (y_prompt, y_sample)
```

```python
import functools
import math

import jax
import jax.numpy as jnp
from jax import lax
from jax.experimental import pallas as pl
from jax.experimental.pallas import tpu as pltpu

F32 = jnp.float32
BF16 = jnp.bfloat16

D_MODEL = 2048
DEPTH = 4
HEAD_DIM = 128
N_Q_HEADS = 8
N_KV_HEADS = 2
Q_PER_KV = N_Q_HEADS // N_KV_HEADS
ATTN_W = N_Q_HEADS * HEAD_DIM
KV_W = N_KV_HEADS * HEAD_DIM
WINDOW = 128
ROPE_DIM = HEAD_DIM // 4
ROPE_THETA = 500000.0
SSM_W = 512
SSM_H = 16
SSM_G = SSM_W // SSM_H
SSM_P = 64
CONV_W = 512
CONV_K = 31
D_MIX = ATTN_W + SSM_W + CONV_W
IN_COLS = ATTN_W + 2 * KV_W + SSM_W + 2 * CONV_W
MEM_LEN = 256
X_HEADS = 4
X_HEAD_DIM = 128
X_W = X_HEADS * X_HEAD_DIM
D_FF = 5632
EPS = 1e-6
NEG = -1e30

O_Q = 0
O_KV = ATTN_W
O_U = ATTN_W + 2 * KV_W
O_C = O_U + SSM_W

SSM_T = 16
SSM_PAIRS = SSM_G // 2
SSM_TH = SSM_T * SSM_H

V7X_VMEM_BYTES = 64 * 1024 * 1024
VMEM_LIMIT = V7X_VMEM_BYTES - 8 * 1024 * 1024
LANES = 128
SUBLANES = 8
CONV_HALO = 2 * SUBLANES


def _cparams(sem):
    return pltpu.CompilerParams(dimension_semantics=sem, vmem_limit_bytes=VMEM_LIMIT)


def _const_spec(shape):
    nd = len(shape)
    return pl.BlockSpec(shape, lambda *_: (0,) * nd)


def _rmsnorm(xf, g):
    ms = jnp.mean(xf * xf, axis=-1, keepdims=True)
    return xf * lax.rsqrt(ms + EPS) * g


def _row_tile(n, pref):
    t = min(n, pref)
    assert n % t == 0, (n, t)
    return t


def _ffn_body(x_ref, g_ref, wg_ref, wu_ref, wd_ref, fg_ref, o_ref, h_ref, *, n_ff, final):
    j = pl.program_id(1)

    @pl.when(j == 0)
    def _():
        h_ref[...] = _rmsnorm(x_ref[...], g_ref[...]).astype(BF16)

    h = h_ref[...]
    g = jnp.dot(h, wg_ref[...], preferred_element_type=F32)
    u = jnp.dot(h, wu_ref[...], preferred_element_type=F32)
    a = (jax.nn.silu(g) * u).astype(BF16)
    d = jnp.dot(a, wd_ref[...], preferred_element_type=F32)

    @pl.when(j == 0)
    def _():
        o_ref[...] = d

    @pl.when(j > 0)
    def _():
        o_ref[...] += d

    @pl.when(j == n_ff - 1)
    def _():
        y = x_ref[...] + 0.5 * o_ref[...]
        if final:
            y = _rmsnorm(y, fg_ref[...])
        o_ref[...] = y


def _ffn(x, gain, wg, wu, wd, final_gain, *, final, tm=512, tf=512):
    n, d = x.shape
    dff = wg.shape[1]
    tm = _row_tile(n, tm)
    tf = _row_tile(dff, tf)
    n_ff = dff // tf
    return pl.pallas_call(
        functools.partial(_ffn_body, n_ff=n_ff, final=final),
        grid=(n // tm, n_ff),
        in_specs=[
            pl.BlockSpec((tm, d), lambda i, j: (i, 0)),
            _const_spec((1, d)),
            pl.BlockSpec((d, tf), lambda i, j: (0, j)),
            pl.BlockSpec((d, tf), lambda i, j: (0, j)),
            pl.BlockSpec((tf, d), lambda i, j: (j, 0)),
            _const_spec((1, d)),
        ],
        out_specs=pl.BlockSpec((tm, d), lambda i, j: (i, 0)),
        out_shape=jax.ShapeDtypeStruct((n, d), F32),
        scratch_shapes=[pltpu.VMEM((tm, d), BF16)],
        compiler_params=_cparams(("parallel", "arbitrary")),
        name="ffn",
    )(x, gain, wg, wu, wd, final_gain)


def _rope_tables(seq_len):
    half = ROPE_DIM // 2
    inv_freq = ROPE_THETA ** (-(2.0 * jnp.arange(half, dtype=F32)) / ROPE_DIM)
    ang = jnp.arange(seq_len, dtype=jnp.int32).astype(F32)[:, None] * inv_freq[None, :]
    cos = jnp.cos(ang)
    sin = jnp.sin(ang)
    zeros = jnp.zeros((seq_len, HEAD_DIM - ROPE_DIM), F32)
    zh = jnp.zeros((seq_len, half), F32)
    cos_t = jnp.concatenate([cos, cos, jnp.ones_like(zeros)], axis=1)
    s1_t = jnp.concatenate([-sin, zh, zeros], axis=1)
    s2_t = jnp.concatenate([zh, sin, zeros], axis=1)
    return cos_t, s1_t, s2_t


def _inproj_body(x_ref, g_ref, w_ref, cos_ref, s1_ref, s2_ref, q_ref, kv_ref, u_ref, c_ref):
    h = _rmsnorm(x_ref[...], g_ref[...]).astype(BF16)
    cos_t, s1_t, s2_t = cos_ref[...], s1_ref[...], s2_ref[...]
    half = ROPE_DIM // 2

    def rope(z):
        return (z * cos_t + pltpu.roll(z, HEAD_DIM - half, 1) * s1_t
                + pltpu.roll(z, half, 1) * s2_t)

    zq = jnp.dot(h, w_ref[:, O_Q:O_KV], preferred_element_type=F32)
    for hd in range(N_Q_HEADS):
        sl = slice(hd * HEAD_DIM, (hd + 1) * HEAD_DIM)
        q_ref[:, sl] = rope(zq[:, sl]).astype(BF16)
    zkv = jnp.dot(h, w_ref[:, O_KV:O_U], preferred_element_type=F32)
    for hd in range(N_KV_HEADS):
        sl = slice(hd * HEAD_DIM, (hd + 1) * HEAD_DIM)
        kv_ref[:, sl] = rope(zkv[:, sl]).astype(BF16)
    kv_ref[:, KV_W:] = zkv[:, KV_W:].astype(BF16)
    u_ref[...] = jnp.dot(h, w_ref[:, O_U:O_C], preferred_element_type=F32)
    c_ref[...] = jnp.dot(h, w_ref[:, O_C:], preferred_element_type=F32)


def _inproj(x, gain, w_in, rope_tabs, seq_len, *, tm=512):
    n, d = x.shape
    tm = _row_tile(seq_len, tm)
    per_seq = seq_len // tm
    tab_spec = pl.BlockSpec((tm, HEAD_DIM), lambda i: (i % per_seq, 0))
    row = lambda w: pl.BlockSpec((tm, w), lambda i: (i, 0))
    return pl.pallas_call(
        _inproj_body,
        grid=(n // tm,),
        in_specs=[row(d), _const_spec((1, d)), _const_spec((d, IN_COLS)), tab_spec, tab_spec, tab_spec],
        out_specs=[row(ATTN_W), row(2 * KV_W), row(SSM_W), row(2 * CONV_W)],
        out_shape=[
            jax.ShapeDtypeStruct((n, ATTN_W), BF16),
            jax.ShapeDtypeStruct((n, 2 * KV_W), BF16),
            jax.ShapeDtypeStruct((n, SSM_W), F32),
            jax.ShapeDtypeStruct((n, 2 * CONV_W), F32),
        ],
        compiler_params=_cparams(("parallel",)),
        name="inproj",
    )(x, gain, w_in, *rope_tabs)


def _attn_body(sink_ref, q_ref, kvc_ref, kvp_ref, kvn_ref, gn_ref, o_ref, acc_ref, *, tq, seq_len):
    i = pl.program_id(1)
    nk = tq + 2 * WINDOW
    t = lax.broadcasted_iota(jnp.int32, (tq, nk), 0)
    c = lax.broadcasted_iota(jnp.int32, (tq, nk), 1)
    lo = jnp.maximum(t, WINDOW - i * tq)
    hi = jnp.minimum(t + 2 * WINDOW, seq_len - i * tq + WINDOW - 1)
    scale = 1.0 / math.sqrt(HEAD_DIM)
    ss = jnp.zeros((tq, 1), F32)
    for hk in range(N_KV_HEADS):
        ks = slice(hk * HEAD_DIM, (hk + 1) * HEAD_DIM)
        vs = slice(KV_W + hk * HEAD_DIM, KV_W + (hk + 1) * HEAD_DIM)
        k = jnp.concatenate([kvp_ref[:, ks], kvc_ref[:, ks], kvn_ref[:, ks]], axis=0)
        v = jnp.concatenate([kvp_ref[:, vs], kvc_ref[:, vs], kvn_ref[:, vs]], axis=0)
        for g in range(Q_PER_KV):
            hq = hk * Q_PER_KV + g
            qs = slice(hq * HEAD_DIM, (hq + 1) * HEAD_DIM)
            s = lax.dot_general(q_ref[:, qs], k, (((1,), (1,)), ((), ())),
                                preferred_element_type=F32) * scale
            s = jnp.where(c >= lo, jnp.where(c <= hi, s, NEG), NEG)
            sk = sink_ref[hq]
            m = jnp.maximum(jnp.max(s, axis=-1, keepdims=True), sk)
            p = jnp.exp(s - m)
            denom = jnp.sum(p, axis=-1, keepdims=True) + jnp.exp(sk - m)
            o = jnp.dot(p.astype(BF16), v, preferred_element_type=F32) / denom
            acc_ref[:, qs] = o
            ss = ss + jnp.sum(o * o, axis=-1, keepdims=True)
    inv = lax.rsqrt(ss * (1.0 / ATTN_W) + EPS)
    o_ref[...] = (acc_ref[...] * inv * gn_ref[...]).astype(BF16)


def _window_attn(q, kv, sink, gain, batch, seq_len, *, tq=256):
    n = q.shape[0]
    tq = _row_tile(seq_len, tq)
    nq = seq_len // tq
    wpt = tq // WINDOW
    n_wblk = n // WINDOW
    return pl.pallas_call(
        functools.partial(_attn_body, tq=tq, seq_len=seq_len),
        grid=(batch, nq),
        in_specs=[
            pl.BlockSpec(memory_space=pltpu.SMEM),
            pl.BlockSpec((tq, ATTN_W), lambda b, i: (b * nq + i, 0)),
            pl.BlockSpec((tq, 2 * KV_W), lambda b, i: (b * nq + i, 0)),
            pl.BlockSpec((WINDOW, 2 * KV_W),
                         lambda b, i: (jnp.maximum((b * nq + i) * wpt - 1, 0), 0)),
            pl.BlockSpec((WINDOW, 2 * KV_W),
                         lambda b, i: (jnp.minimum((b * nq + i + 1) * wpt, n_wblk - 1), 0)),
            _const_spec((1, ATTN_W)),
        ],
        out_specs=pl.BlockSpec((tq, ATTN_W), lambda b, i: (b * nq + i, 0)),
        out_shape=jax.ShapeDtypeStruct((n, ATTN_W), BF16),
        scratch_shapes=[pltpu.VMEM((tq, ATTN_W), F32)],
        compiler_params=_cparams(("parallel", "parallel")),
        name="window_attn",
    )(sink, q, kv, kv, kv, gain)


def _s5_operators(lam_re, lam_im, log_dt, b_re, b_im, c_re, c_im, nlev):
    hp = lax.Precision.HIGHEST
    T, H, P, G = SSM_T, SSM_H, SSM_P, SSM_G
    lr = jnp.minimum(lam_re.astype(F32), -1e-4)
    li = lam_im.astype(F32)
    dt = jnp.exp(log_dt.astype(F32))[..., None]
    lg = lr * dt
    th = li * dt

    def cpow(tau):
        tau = tau[:, None, None, None]
        mag = jnp.exp(lg * tau)
        return mag * jnp.cos(th * tau), mag * jnp.sin(th * tau)

    ab_re, ab_im = (v[0] for v in cpow(jnp.ones((1,), F32)))
    den = lr * lr + li * li
    n_re = ab_re - 1.0
    n_im = ab_im
    f_re = (n_re * lr + n_im * li) / den
    f_im = (n_im * lr - n_re * li) / den
    br = b_re.astype(F32)
    bi = b_im.astype(F32)
    bb_re = f_re[..., None] * br - f_im[..., None] * bi
    bb_im = f_re[..., None] * bi + f_im[..., None] * br
    cr = c_re.astype(F32)
    ci = c_im.astype(F32)

    pw_re, pw_im = cpow(jnp.arange(T + 1, dtype=F32))
    w_re = pw_re[..., None] * bb_re - pw_im[..., None] * bb_im
    w_im = pw_re[..., None] * bb_im + pw_im[..., None] * bb_re
    kmat = (jnp.einsum('dgop,tdgpi->tdgoi', cr, w_re, precision=hp)
            - jnp.einsum('dgop,tdgpi->tdgoi', ci, w_im, precision=hp))

    s_idx = jnp.arange(T)[:, None]
    t_idx = jnp.arange(T)[None, :]
    kf = kmat[jnp.clip(t_idx - s_idx, 0, T), 0]
    kb = kmat[jnp.clip(s_idx - t_idx, 0, T), 1]
    tri_f = (t_idx >= s_idx)[:, :, None, None, None]
    tri_b = (s_idx >= t_idx)[:, :, None, None, None]
    m_intra = jnp.where(tri_f, kf, 0.0) + jnp.where(tri_b, kb, 0.0)
    m_intra = m_intra.transpose(2, 0, 4, 1, 3).reshape(G, T * H, T * H)

    def to_in(wd):
        return wd.transpose(1, 0, 3, 2).reshape(G, T * H, P)

    rev = jnp.arange(T - 1, -1, -1)
    fwd = jnp.arange(T)
    m_in = jnp.stack([to_in(w_re[rev, 0]), to_in(w_im[rev, 0]),
                      to_in(w_re[fwd, 1]), to_in(w_im[fwd, 1])], axis=2)

    def to_out(pr, pi, d):
        wr = cr[d][None] * pr[:, :, None, :] - ci[d][None] * pi[:, :, None, :]
        wi = cr[d][None] * pi[:, :, None, :] + ci[d][None] * pr[:, :, None, :]
        tr = lambda a: a.transpose(1, 3, 0, 2).reshape(G, P, T * H)
        return tr(wr), tr(-wi)

    of_re, of_im = to_out(pw_re[1:, 0], pw_im[1:, 0], 0)
    ob_re, ob_im = to_out(pw_re[T - fwd, 1], pw_im[T - fwd, 1], 1)
    m_out = jnp.stack([of_re, of_im, ob_re, ob_im], axis=1)

    eye = jnp.eye(2, dtype=F32)
    J = SSM_PAIRS
    w_in = jnp.einsum('jerqp,ef->jerqfp', m_in.reshape(J, 2, T * H, 4, P), eye)
    w_in = w_in.reshape(J, 2 * T * H, 8 * P)
    w_u = jnp.einsum('jers,ef->jerfs', m_intra.reshape(J, 2, T * H, T * H), eye)
    w_u = w_u.reshape(J, 2 * T * H, 2 * T * H)
    w_s = jnp.einsum('jeqps,ef->jqepfs', m_out.reshape(J, 2, 4, P, T * H), eye)
    w_s = w_s.reshape(J, 8 * P, 2 * T * H)
    w_out = jnp.concatenate([w_u, w_s], axis=1)

    lv_re, lv_im = cpow(jnp.asarray([float(T * 2 ** k) for k in range(nlev)], F32))
    lv = jnp.stack([lv_re, lv_im], axis=2)
    lv = lv.reshape(nlev, 2, 2, J, 2 * P).transpose(3, 1, 2, 0, 4).reshape(J, 4, nlev, 2 * P)
    return w_in.astype(BF16), w_out.astype(BF16), lv


def _s5_body(u_ref, win_ref, wout_ref, lv_ref, y_ref, *, n_chunks, nlev):
    u = u_ref[...]
    rows = u.shape[0]
    sw = 2 * SSM_P
    x = jnp.dot(u, win_ref[0], preferred_element_type=F32)
    pos = lax.broadcasted_iota(jnp.int32, (rows, sw), 0) & (n_chunks - 1)

    def shifted(v, shift, fwd):
        if fwd:
            return jnp.where(pos >= shift, pltpu.roll(v, shift, 0), 0.0)
        return jnp.where(pos < n_chunks - shift, pltpu.roll(v, rows - shift, 0), 0.0)

    states = []
    for d, fwd in ((0, True), (1, False)):
        s_re = x[:, (2 * d) * sw:(2 * d + 1) * sw]
        s_im = x[:, (2 * d + 1) * sw:(2 * d + 2) * sw]
        for k in range(nlev):
            shift = 2 ** k
            if shift >= n_chunks:
                break
            a_re = lv_ref[0, 2 * d, k:k + 1, :]
            a_im = lv_ref[0, 2 * d + 1, k:k + 1, :]
            p_re = shifted(s_re, shift, fwd)
            p_im = shifted(s_im, shift, fwd)
            s_re, s_im = (s_re + a_re * p_re - a_im * p_im,
                          s_im + a_re * p_im + a_im * p_re)
        states.append(shifted(s_re, 1, fwd).astype(BF16))
        states.append(shifted(s_im, 1, fwd).astype(BF16))
    lhs = jnp.concatenate([u] + states, axis=1)
    y_ref[...] = jnp.dot(lhs, wout_ref[0], preferred_element_type=F32)


def _s5_scan(u_chunks, w_in, w_out, lv, seq_len, *, rows=512):
    nrow = u_chunks.shape[0]
    n_chunks = seq_len // SSM_T
    rows = max(min(rows, nrow), n_chunks)
    assert nrow % rows == 0 and rows % n_chunks == 0
    assert n_chunks & (n_chunks - 1) == 0
    nlev = lv.shape[2]
    assert 2 ** nlev >= n_chunks
    pw = 2 * SSM_TH
    return pl.pallas_call(
        functools.partial(_s5_body, n_chunks=n_chunks, nlev=nlev),
        grid=(SSM_PAIRS, nrow // rows),
        in_specs=[
            pl.BlockSpec((rows, pw), lambda j, r: (r, j)),
            pl.BlockSpec((1, pw, 8 * SSM_P), lambda j, r: (j, 0, 0)),
            pl.BlockSpec((1, pw + 8 * SSM_P, pw), lambda j, r: (j, 0, 0)),
            pl.BlockSpec((1, 4, nlev, 2 * SSM_P), lambda j, r: (j, 0, 0, 0)),
        ],
        out_specs=pl.BlockSpec((rows, pw), lambda j, r: (r, j)),
        out_shape=jax.ShapeDtypeStruct((nrow, SSM_G * SSM_TH), F32),
        compiler_params=_cparams(("parallel", "parallel")),
        name="s5_scan",
    )(u_chunks, w_in, w_out, lv)


def _s5_post_body(y_ref, u_ref, d_ref, w_ref, b_ref, gn_ref, o_ref):
    y = y_ref[...] + d_ref[...] * u_ref[...]
    g = jax.nn.gelu(y)
    z = jnp.dot(g.astype(BF16), w_ref[...], preferred_element_type=F32) + b_ref[...]
    s = g * jax.nn.sigmoid(z)
    o_ref[...] = _rmsnorm(s, gn_ref[...]).astype(BF16)


def _s5_post(y, u, d_skip, w_glu, b_glu, gain, *, tm=1024):
    n = y.shape[0]
    tm = _row_tile(n, tm)
    row = pl.BlockSpec((tm, SSM_W), lambda i: (i, 0))
    vec = _const_spec((1, SSM_W))
    return pl.pallas_call(
        _s5_post_body,
        grid=(n // tm,),
        in_specs=[row, row, vec, _const_spec((SSM_W, SSM_W)), vec, vec],
        out_specs=row,
        out_shape=jax.ShapeDtypeStruct((n, SSM_W), BF16),
        compiler_params=_cparams(("parallel",)),
        name="s5_post",
    )(y, u, d_skip, w_glu, b_glu, gain)


def _conv_body(cc_ref, cp_ref, cn_ref, w_ref, b_ref, lng_ref, lnb_ref, gn_ref, o_ref, gbuf,
               *, tl, nblk, rb):
    i = pl.program_id(1)

    def glu(v):
        return v[:, :CONV_W] * jax.nn.sigmoid(v[:, CONV_W:])

    gbuf[0:CONV_HALO, :] = jnp.where(i > 0, glu(cp_ref[...]), 0.0)
    gbuf[CONV_HALO:CONV_HALO + tl, :] = glu(cc_ref[...])
    gbuf[CONV_HALO + tl:2 * CONV_HALO + tl, :] = jnp.where(i < nblk - 1, glu(cn_ref[...]), 0.0)
    base = CONV_HALO - CONV_K // 2
    for r0 in range(0, tl, rb):
        acc = jnp.broadcast_to(b_ref[...], (rb, CONV_W))
        for k in range(CONV_K):
            acc = acc + gbuf[r0 + base + k:r0 + base + k + rb, :] * w_ref[k:k + 1, :]
        mu = jnp.mean(acc, axis=-1, keepdims=True)
        xc = acc - mu
        var = jnp.mean(xc * xc, axis=-1, keepdims=True)
        y = xc * lax.rsqrt(var + EPS) * lng_ref[...] + lnb_ref[...]
        y = jax.nn.silu(y)
        o_ref[r0:r0 + rb, :] = _rmsnorm(y, gn_ref[...]).astype(BF16)


def _conformer_conv(c, conv_w, conv_b, ln_g, ln_b, gain, batch, seq_len, *, tl=512, rb=32):
    n = c.shape[0]
    tl = _row_tile(seq_len, tl)
    rb = _row_tile(tl, rb)
    nblk = seq_len // tl
    hpt = tl // CONV_HALO
    n_hblk = n // CONV_HALO
    vec = _const_spec((1, CONV_W))
    return pl.pallas_call(
        functools.partial(_conv_body, tl=tl, nblk=nblk, rb=rb),
        grid=(batch, nblk),
        in_specs=[
            pl.BlockSpec((tl, 2 * CONV_W), lambda b, i: (b * nblk + i, 0)),
            pl.BlockSpec((CONV_HALO, 2 * CONV_W),
                         lambda b, i: (jnp.maximum((b * nblk + i) * hpt - 1, 0), 0)),
            pl.BlockSpec((CONV_HALO, 2 * CONV_W),
                         lambda b, i: (jnp.minimum((b * nblk + i + 1) * hpt, n_hblk - 1), 0)),
            _const_spec((CONV_K, CONV_W)),
            vec, vec, vec, vec,
        ],
        out_specs=pl.BlockSpec((tl, CONV_W), lambda b, i: (b * nblk + i, 0)),
        out_shape=jax.ShapeDtypeStruct((n, CONV_W), BF16),
        scratch_shapes=[pltpu.VMEM((tl + 2 * CONV_HALO, CONV_W), F32)],
        compiler_params=_cparams(("parallel", "parallel")),
        name="conformer_conv",
    )(c, c, c, conv_w, conv_b, ln_g, ln_b, gain)


def _memkv_body(m_ref, g_ref, w_ref, o_ref):
    h = _rmsnorm(m_ref[...], g_ref[...]).astype(BF16)
    o_ref[...] = jnp.dot(h, w_ref[...], preferred_element_type=F32).astype(BF16)


def _mem_kv(mem, gain, w_kv, *, tm=512):
    n, d = mem.shape
    tm = _row_tile(n, tm)
    return pl.pallas_call(
        _memkv_body,
        grid=(n // tm,),
        in_specs=[pl.BlockSpec((tm, d), lambda i: (i, 0)), _const_spec((1, d)),
                  _const_spec((d, 2 * X_W))],
        out_specs=pl.BlockSpec((tm, 2 * X_W), lambda i: (i, 0)),
        out_shape=jax.ShapeDtypeStruct((n, 2 * X_W), BF16),
        compiler_params=_cparams(("parallel",)),
        name="mem_kv",
    )(mem, gain, w_kv)


def _mixout_body(x_ref, a_ref, s_ref, c_ref, wout_ref, gx_ref, wq_ref, kv_ref, wo_ref, o_ref):
    mixed = jnp.concatenate([a_ref[...], s_ref[...], c_ref[...]], axis=1)
    x1 = x_ref[...] + jnp.dot(mixed, wout_ref[...], preferred_element_type=F32)
    h = _rmsnorm(x1, gx_ref[...]).astype(BF16)
    q = jnp.dot(h, wq_ref[...], preferred_element_type=F32).astype(BF16)
    scale = 1.0 / math.sqrt(X_HEAD_DIM)
    heads = []
    for hd in range(X_HEADS):
        ks = slice(hd * X_HEAD_DIM, (hd + 1) * X_HEAD_DIM)
        vs = slice(X_W + hd * X_HEAD_DIM, X_W + (hd + 1) * X_HEAD_DIM)
        s = lax.dot_general(q[:, ks], kv_ref[:, ks], (((1,), (1,)), ((), ())),
                            preferred_element_type=F32) * scale
        m = jnp.max(s, axis=-1, keepdims=True)
        p = jnp.exp(s - m)
        denom = jnp.sum(p, axis=-1, keepdims=True)
        o = jnp.dot(p.astype(BF16), kv_ref[:, vs], preferred_element_type=F32) / denom
        heads.append(o.astype(BF16))
    o = jnp.concatenate(heads, axis=1)
    o_ref[...] = x1 + jnp.dot(o, wo_ref[...], preferred_element_type=F32)


def _mix_out(x, a, s, c, w_out, gain_x, w_q, kv_mem, w_o, batch, seq_len, *, tm=512):
    n, d = x.shape
    tm = _row_tile(seq_len, tm)
    per_seq = seq_len // tm
    row = lambda w: pl.BlockSpec((tm, w), lambda b, i: (b * per_seq + i, 0))
    return pl.pallas_call(
        _mixout_body,
        grid=(batch, per_seq),
        in_specs=[
            row(d), row(ATTN_W), row(SSM_W), row(CONV_W),
            _const_spec((D_MIX, d)), _const_spec((1, d)), _const_spec((d, X_W)),
            pl.BlockSpec((MEM_LEN, 2 * X_W), lambda b, i: (b, 0)),
            _const_spec((X_W, d)),
        ],
        out_specs=row(d),
        out_shape=jax.ShapeDtypeStruct((n, d), F32),
        compiler_params=_cparams(("parallel", "parallel")),
        name="mix_out",
    )(x, a, s, c, w_out, gain_x, w_q, kv_mem, w_o)


def _to_chunks(u):
    n = u.shape[0]
    uc = u.reshape(n // SSM_T, SSM_T, SSM_G, SSM_H).transpose(0, 2, 1, 3)
    return uc.reshape(n // SSM_T, SSM_G * SSM_TH).astype(BF16)


def _from_chunks(y):
    nrow = y.shape[0]
    yc = y.reshape(nrow, SSM_G, SSM_T, SSM_H).transpose(0, 2, 1, 3)
    return yc.reshape(nrow * SSM_T, SSM_W)


def _encoder_layer(x, mem, p, rope_tabs, batch, seq_len, final_gain, final):
    vec = lambda a: a.reshape(1, -1).astype(F32)
    x = _ffn(x, vec(p['ffn1_norm']), p['ffn1_w_gate'], p['ffn1_w_up'], p['ffn1_w_down'],
             final_gain, final=False)
    q, kv, u, c = _inproj(x, vec(p['mix_norm']), p['w_in'], rope_tabs, seq_len)
    a_out = _window_attn(q, kv, p['attn_sink'].astype(F32), vec(p['attn_out_norm']), batch, seq_len)
    y = _s5_scan(_to_chunks(u), p['s5_w_in'], p['s5_w_out'], p['s5_lv'], seq_len)
    s_out = _s5_post(_from_chunks(y), u, vec(p['ssm_d']), p['ssm_w_glu'], vec(p['ssm_b_glu']),
                     vec(p['ssm_out_norm']))
    c_out = _conformer_conv(c, p['conv_w'].astype(F32), vec(p['conv_b']), vec(p['conv_ln_g']),
                            vec(p['conv_ln_b']), vec(p['conv_out_norm']), batch, seq_len)
    kv_mem = _mem_kv(mem, vec(p['mem_norm']), p['xattn_w_kv'])
    x = _mix_out(x, a_out, s_out, c_out, p['w_out'], vec(p['xattn_norm']), p['xattn_w_q'], kv_mem,
                 p['xattn_w_o'], batch, seq_len)
    x = _ffn(x, vec(p['ffn2_norm']), p['ffn2_w_gate'], p['ffn2_w_up'], p['ffn2_w_down'],
             final_gain, final=final)
    return x


_MATMUL_WEIGHTS = ('ffn1_w_gate', 'ffn1_w_up', 'ffn1_w_down', 'w_in', 'ssm_w_glu', 'w_out',
                   'xattn_w_q', 'xattn_w_kv', 'xattn_w_o', 'ffn2_w_gate', 'ffn2_w_up', 'ffn2_w_down')


def _trunk(groups, stacked, final_norm):
    depth = stacked['w_in'].shape[0]
    final_gain = final_norm.reshape(1, -1).astype(F32)
    max_len = max(x.shape[1] for x, _ in groups)
    nlev = max(1, (max_len // SSM_T - 1).bit_length())
    state = []
    for x, mem in groups:
        b, l, d = x.shape
        state.append([x.reshape(b * l, d), mem.reshape(-1, d), _rope_tables(l), b, l])
    for layer in range(depth):
        p = {k: v[layer] for k, v in stacked.items()}
        for k in _MATMUL_WEIGHTS:
            p[k] = p[k].astype(BF16)
        p['s5_w_in'], p['s5_w_out'], p['s5_lv'] = _s5_operators(
            p['ssm_lambda_re'], p['ssm_lambda_im'], p['ssm_log_dt'], p['ssm_b_re'], p['ssm_b_im'],
            p['ssm_c_re'], p['ssm_c_im'], nlev)
        for st in state:
            st[0] = _encoder_layer(st[0], st[1], p, st[2], st[3], st[4], final_gain,
                                   final=(layer == depth - 1))
    return tuple(st[0].reshape(st[3], st[4], -1) for st in state)


def kernel(x_prompt, x_sample, mem_prompt, mem_sample, ffn1_norm, ffn1_w_gate, ffn1_w_up, ffn1_w_down, mix_norm, w_in, attn_sink, ssm_lambda_re, ssm_lambda_im, ssm_log_dt, ssm_b_re, ssm_b_im, ssm_c_re, ssm_c_im, ssm_d, ssm_w_glu, ssm_b_glu, conv_w, conv_b, conv_ln_g, conv_ln_b, attn_out_norm, ssm_out_norm, conv_out_norm, w_out, xattn_norm, mem_norm, xattn_w_q, xattn_w_kv, xattn_w_o, ffn2_norm, ffn2_w_gate, ffn2_w_up, ffn2_w_down, final_norm):
    stacked = {
        'ffn1_norm': ffn1_norm, 'ffn1_w_gate': ffn1_w_gate, 'ffn1_w_up': ffn1_w_up, 'ffn1_w_down': ffn1_w_down,
        'mix_norm': mix_norm, 'w_in': w_in, 'attn_sink': attn_sink,
        'ssm_lambda_re': ssm_lambda_re, 'ssm_lambda_im': ssm_lambda_im, 'ssm_log_dt': ssm_log_dt,
        'ssm_b_re': ssm_b_re, 'ssm_b_im': ssm_b_im, 'ssm_c_re': ssm_c_re, 'ssm_c_im': ssm_c_im,
        'ssm_d': ssm_d, 'ssm_w_glu': ssm_w_glu, 'ssm_b_glu': ssm_b_glu,
        'conv_w': conv_w, 'conv_b': conv_b, 'conv_ln_g': conv_ln_g, 'conv_ln_b': conv_ln_b,
        'attn_out_norm': attn_out_norm, 'ssm_out_norm': ssm_out_norm, 'conv_out_norm': conv_out_norm,
        'w_out': w_out,
        'xattn_norm': xattn_norm, 'mem_norm': mem_norm,
        'xattn_w_q': xattn_w_q, 'xattn_w_kv': xattn_w_kv, 'xattn_w_o': xattn_w_o,
        'ffn2_norm': ffn2_norm, 'ffn2_w_gate': ffn2_w_gate, 'ffn2_w_up': ffn2_w_up, 'ffn2_w_down': ffn2_w_down,
    }
    y_prompt, y_sample = _trunk([(x_prompt, mem_prompt), (x_sample, mem_sample)], stacked, final_norm)
    return (y_prompt, y_sample)
```

```python
import functools
import math

import jax
import jax.numpy as jnp
from jax import lax
from jax.experimental import pallas as pl
from jax.experimental.pallas import tpu as pltpu

F32 = jnp.float32
BF16 = jnp.bfloat16

D_MODEL = 2048
DEPTH = 4
HEAD_DIM = 128
N_Q_HEADS = 8
N_KV_HEADS = 2
Q_PER_KV = N_Q_HEADS // N_KV_HEADS
ATTN_W = N_Q_HEADS * HEAD_DIM
KV_W = N_KV_HEADS * HEAD_DIM
WINDOW = 128
ROPE_DIM = HEAD_DIM // 4
ROPE_THETA = 500000.0
SSM_W = 512
SSM_H = 16
SSM_G = SSM_W // SSM_H
SSM_P = 64
CONV_W = 512
CONV_K = 31
D_MIX = ATTN_W + SSM_W + CONV_W
IN_COLS = ATTN_W + 2 * KV_W + SSM_W + 2 * CONV_W
MEM_LEN = 256
X_HEADS = 4
X_HEAD_DIM = 128
X_W = X_HEADS * X_HEAD_DIM
D_FF = 5632
EPS = 1e-6
NEG = -1e30

O_Q = 0
O_KV = ATTN_W
O_U = ATTN_W + 2 * KV_W
O_C = O_U + SSM_W

SSM_T = 16
SSM_PAIRS = SSM_G // 2
SSM_TH = SSM_T * SSM_H

V7X_VMEM_BYTES = 64 * 1024 * 1024
VMEM_LIMIT = V7X_VMEM_BYTES - 8 * 1024 * 1024
LANES = 128
SUBLANES = 8
CONV_HALO = 2 * SUBLANES


def _cparams(sem):
    return pltpu.CompilerParams(dimension_semantics=sem, vmem_limit_bytes=VMEM_LIMIT)


def _const_spec(shape):
    nd = len(shape)
    return pl.BlockSpec(shape, lambda *_: (0,) * nd)


def _rmsnorm(xf, g):
    ms = jnp.mean(xf * xf, axis=-1, keepdims=True)
    return xf * lax.rsqrt(ms + EPS) * g


def _row_tile(n, pref):
    t = min(n, pref)
    assert n % t == 0, (n, t)
    return t


def _ffn_body(x_ref, g_ref, wg_ref, wu_ref, wd_ref, fg_ref, o_ref, h_ref, *, n_ff, final):
    j = pl.program_id(1)

    @pl.when(j == 0)
    def _():
        tm = o_ref.shape[0]
        rc = min(tm, 2 * LANES)
        for r0 in range(0, tm, rc):
            h_ref[r0:r0 + rc, :] = _rmsnorm(x_ref[r0:r0 + rc, :], g_ref[...]).astype(BF16)
        o_ref[...] = jnp.zeros_like(o_ref)

    h = h_ref[...]
    g = jnp.dot(h, wg_ref[...], preferred_element_type=F32)
    u = jnp.dot(h, wu_ref[...], preferred_element_type=F32)
    a = (jax.nn.silu(g) * u).astype(BF16)
    o_ref[...] += jnp.dot(a, wd_ref[...], preferred_element_type=F32)

    @pl.when(j == n_ff - 1)
    def _():
        tm = o_ref.shape[0]
        rc = min(tm, LANES)

        def chunk(r, carry):
            rows = pl.ds(pl.multiple_of(r * rc, rc), rc)
            y = x_ref[rows, :] + 0.5 * o_ref[rows, :]
            if final:
                y = _rmsnorm(y, fg_ref[...])
            o_ref[rows, :] = y
            return carry

        lax.fori_loop(0, tm // rc, chunk, 0)


def _ffn(x, gain, wg, wu, wd, final_gain, *, final, tm=1024, tf=512):
    n, d = x.shape
    dff = wg.shape[1]
    tm = _row_tile(n, tm)
    tf = _row_tile(dff, tf)
    n_ff = dff // tf
    return pl.pallas_call(
        functools.partial(_ffn_body, n_ff=n_ff, final=final),
        grid=(n // tm, n_ff),
        in_specs=[
            pl.BlockSpec((tm, d), lambda i, j: (i, 0)),
            _const_spec((1, d)),
            pl.BlockSpec((d, tf), lambda i, j: (0, j)),
            pl.BlockSpec((d, tf), lambda i, j: (0, j)),
            pl.BlockSpec((tf, d), lambda i, j: (j, 0)),
            _const_spec((1, d)),
        ],
        out_specs=pl.BlockSpec((tm, d), lambda i, j: (i, 0)),
        out_shape=jax.ShapeDtypeStruct((n, d), F32),
        scratch_shapes=[pltpu.VMEM((tm, d), BF16)],
        compiler_params=_cparams(("parallel", "arbitrary")),
        name="ffn",
    )(x, gain, wg, wu, wd, final_gain)


def _rope_tables(seq_len):
    half = ROPE_DIM // 2
    inv_freq = ROPE_THETA ** (-(2.0 * jnp.arange(half, dtype=F32)) / ROPE_DIM)
    ang = jnp.arange(seq_len, dtype=jnp.int32).astype(F32)[:, None] * inv_freq[None, :]
    cos = jnp.cos(ang)
    sin = jnp.sin(ang)
    zeros = jnp.zeros((seq_len, HEAD_DIM - ROPE_DIM), F32)
    zh = jnp.zeros((seq_len, half), F32)
    cos_t = jnp.concatenate([cos, cos, jnp.ones_like(zeros)], axis=1)
    s1_t = jnp.concatenate([-sin, zh, zeros], axis=1)
    s2_t = jnp.concatenate([zh, sin, zeros], axis=1)
    return cos_t, s1_t, s2_t


def _inproj_body(x_ref, g_ref, w_ref, cos_ref, s1_ref, s2_ref, q_ref, kv_ref, u_ref, c_ref):
    h = _rmsnorm(x_ref[...], g_ref[...]).astype(BF16)
    cos_t, s1_t, s2_t = cos_ref[...], s1_ref[...], s2_ref[...]
    half = ROPE_DIM // 2

    def rope(z):
        return (z * cos_t + pltpu.roll(z, HEAD_DIM - half, 1) * s1_t
                + pltpu.roll(z, half, 1) * s2_t)

    zq = jnp.dot(h, w_ref[:, O_Q:O_KV], preferred_element_type=F32)
    for hd in range(N_Q_HEADS):
        sl = slice(hd * HEAD_DIM, (hd + 1) * HEAD_DIM)
        q_ref[:, sl] = rope(zq[:, sl]).astype(BF16)
    zkv = jnp.dot(h, w_ref[:, O_KV:O_U], preferred_element_type=F32)
    for hd in range(N_KV_HEADS):
        sl = slice(hd * HEAD_DIM, (hd + 1) * HEAD_DIM)
        kv_ref[:, sl] = rope(zkv[:, sl]).astype(BF16)
    kv_ref[:, KV_W:] = zkv[:, KV_W:].astype(BF16)
    u_ref[...] = jnp.dot(h, w_ref[:, O_U:O_C], preferred_element_type=F32)
    c_ref[...] = jnp.dot(h, w_ref[:, O_C:], preferred_element_type=F32)


def _inproj(x, gain, w_in, rope_tabs, seq_len, *, tm=512):
    n, d = x.shape
    tm = _row_tile(seq_len, tm)
    per_seq = seq_len // tm
    tab_spec = pl.BlockSpec((tm, HEAD_DIM), lambda i: (i % per_seq, 0))
    row = lambda w: pl.BlockSpec((tm, w), lambda i: (i, 0))
    return pl.pallas_call(
        _inproj_body,
        grid=(n // tm,),
        in_specs=[row(d), _const_spec((1, d)), _const_spec((d, IN_COLS)), tab_spec, tab_spec, tab_spec],
        out_specs=[row(ATTN_W), row(2 * KV_W), row(SSM_W), row(2 * CONV_W)],
        out_shape=[
            jax.ShapeDtypeStruct((n, ATTN_W), BF16),
            jax.ShapeDtypeStruct((n, 2 * KV_W), BF16),
            jax.ShapeDtypeStruct((n, SSM_W), F32),
            jax.ShapeDtypeStruct((n, 2 * CONV_W), F32),
        ],
        compiler_params=_cparams(("parallel",)),
        name="inproj",
    )(x, gain, w_in, *rope_tabs)


def _attn_body(sink_ref, q_ref, kvc_ref, kvp_ref, kvn_ref, gn_ref, o_ref, kvbuf, acc_ref,
               *, tq, seq_len):
    i = pl.program_id(1)
    w = WINDOW
    nk = 3 * w
    kvbuf[0:w, :] = kvp_ref[...]
    kvbuf[w:w + tq, :] = kvc_ref[...]
    kvbuf[w + tq:, :] = kvn_ref[...]
    t = lax.broadcasted_iota(jnp.int32, (w, nk), 0)
    c = lax.broadcasted_iota(jnp.int32, (w, nk), 1)
    scale = 1.0 / math.sqrt(HEAD_DIM)
    for sb in range(tq // w):
        p0 = i * tq + sb * w
        lo = jnp.maximum(t, w - p0)
        hi = jnp.minimum(t + 2 * w, seq_len - p0 + w - 1)
        bias = jnp.where(c >= lo, jnp.where(c <= hi, 0.0, NEG), NEG)
        rows = slice(sb * w, (sb + 1) * w)
        krows = slice(sb * w, sb * w + nk)
        for hk in range(N_KV_HEADS):
            k = kvbuf[krows, hk * HEAD_DIM:(hk + 1) * HEAD_DIM]
            v = kvbuf[krows, KV_W + hk * HEAD_DIM:KV_W + (hk + 1) * HEAD_DIM]
            heads = range(hk * Q_PER_KV, (hk + 1) * Q_PER_KV)
            qs = jnp.concatenate([q_ref[rows, h * HEAD_DIM:(h + 1) * HEAD_DIM] for h in heads], axis=0)
            s = lax.dot_general(qs, k, (((1,), (1,)), ((), ())), preferred_element_type=F32) * scale
            ps, dens = [], []
            for g, h in enumerate(heads):
                sg = s[g * w:(g + 1) * w] + bias
                sk = sink_ref[h]
                m = jnp.maximum(jnp.max(sg, axis=-1, keepdims=True), sk)
                p = jnp.exp(sg - m)
                dens.append(jnp.sum(p, axis=-1, keepdims=True) + jnp.exp(sk - m))
                ps.append(p.astype(BF16))
            o = jnp.dot(jnp.concatenate(ps, axis=0), v, preferred_element_type=F32)
            for g, h in enumerate(heads):
                acc_ref[rows, h * HEAD_DIM:(h + 1) * HEAD_DIM] = o[g * w:(g + 1) * w] / dens[g]
    o_ref[...] = _rmsnorm(acc_ref[...], gn_ref[...]).astype(BF16)


def _window_attn(q, kv, sink, gain, batch, seq_len, *, tq=512):
    n = q.shape[0]
    tq = _row_tile(seq_len, tq)
    nq = seq_len // tq
    wpt = tq // WINDOW
    n_wblk = n // WINDOW
    return pl.pallas_call(
        functools.partial(_attn_body, tq=tq, seq_len=seq_len),
        grid=(batch, nq),
        in_specs=[
            pl.BlockSpec(memory_space=pltpu.SMEM),
            pl.BlockSpec((tq, ATTN_W), lambda b, i: (b * nq + i, 0)),
            pl.BlockSpec((tq, 2 * KV_W), lambda b, i: (b * nq + i, 0)),
            pl.BlockSpec((WINDOW, 2 * KV_W),
                         lambda b, i: (jnp.maximum((b * nq + i) * wpt - 1, 0), 0)),
            pl.BlockSpec((WINDOW, 2 * KV_W),
                         lambda b, i: (jnp.minimum((b * nq + i + 1) * wpt, n_wblk - 1), 0)),
            _const_spec((1, ATTN_W)),
        ],
        out_specs=pl.BlockSpec((tq, ATTN_W), lambda b, i: (b * nq + i, 0)),
        out_shape=jax.ShapeDtypeStruct((n, ATTN_W), BF16),
        scratch_shapes=[pltpu.VMEM((tq + 2 * WINDOW, 2 * KV_W), BF16),
                        pltpu.VMEM((tq, ATTN_W), F32)],
        compiler_params=_cparams(("parallel", "parallel")),
        name="window_attn",
    )(sink, q, kv, kv, kv, gain)


def _s5_operators(lam_re, lam_im, log_dt, b_re, b_im, c_re, c_im, nlev):
    hp = lax.Precision.HIGHEST
    T, H, P, G = SSM_T, SSM_H, SSM_P, SSM_G
    lr = jnp.minimum(lam_re.astype(F32), -1e-4)
    li = lam_im.astype(F32)
    dt = jnp.exp(log_dt.astype(F32))[..., None]
    lg = lr * dt
    th = li * dt

    def cpow(tau):
        tau = tau[:, None, None, None]
        mag = jnp.exp(lg * tau)
        return mag * jnp.cos(th * tau), mag * jnp.sin(th * tau)

    ab_re, ab_im = (v[0] for v in cpow(jnp.ones((1,), F32)))
    den = lr * lr + li * li
    n_re = ab_re - 1.0
    n_im = ab_im
    f_re = (n_re * lr + n_im * li) / den
    f_im = (n_im * lr - n_re * li) / den
    br = b_re.astype(F32)
    bi = b_im.astype(F32)
    bb_re = f_re[..., None] * br - f_im[..., None] * bi
    bb_im = f_re[..., None] * bi + f_im[..., None] * br
    cr = c_re.astype(F32)
    ci = c_im.astype(F32)

    pw_re, pw_im = cpow(jnp.arange(T + 1, dtype=F32))
    w_re = pw_re[..., None] * bb_re - pw_im[..., None] * bb_im
    w_im = pw_re[..., None] * bb_im + pw_im[..., None] * bb_re
    kmat = (jnp.einsum('dgop,tdgpi->tdgoi', cr, w_re, precision=hp)
            - jnp.einsum('dgop,tdgpi->tdgoi', ci, w_im, precision=hp))

    s_idx = jnp.arange(T)[:, None]
    t_idx = jnp.arange(T)[None, :]
    kf = kmat[jnp.clip(t_idx - s_idx, 0, T), 0]
    kb = kmat[jnp.clip(s_idx - t_idx, 0, T), 1]
    tri_f = (t_idx >= s_idx)[:, :, None, None, None]
    tri_b = (s_idx >= t_idx)[:, :, None, None, None]
    m_intra = jnp.where(tri_f, kf, 0.0) + jnp.where(tri_b, kb, 0.0)
    m_intra = m_intra.transpose(2, 0, 4, 1, 3).reshape(G, T * H, T * H)

    def to_in(wd):
        return wd.transpose(1, 0, 3, 2).reshape(G, T * H, P)

    rev = jnp.arange(T - 1, -1, -1)
    fwd = jnp.arange(T)
    m_in = jnp.stack([to_in(w_re[rev, 0]), to_in(w_im[rev, 0]),
                      to_in(w_re[fwd, 1]), to_in(w_im[fwd, 1])], axis=2)

    def to_out(pr, pi, d):
        wr = cr[d][None] * pr[:, :, None, :] - ci[d][None] * pi[:, :, None, :]
        wi = cr[d][None] * pi[:, :, None, :] + ci[d][None] * pr[:, :, None, :]
        tr = lambda a: a.transpose(1, 3, 0, 2).reshape(G, P, T * H)
        return tr(wr), tr(-wi)

    of_re, of_im = to_out(pw_re[1:, 0], pw_im[1:, 0], 0)
    ob_re, ob_im = to_out(pw_re[T - fwd, 1], pw_im[T - fwd, 1], 1)
    m_out = jnp.stack([of_re, of_im, ob_re, ob_im], axis=1)

    eye = jnp.eye(2, dtype=F32)
    J = SSM_PAIRS
    w_in = jnp.einsum('jerqp,ef->jerqfp', m_in.reshape(J, 2, T * H, 4, P), eye)
    w_in = w_in.reshape(J, 2 * T * H, 8 * P)
    w_u = jnp.einsum('jers,ef->jerfs', m_intra.reshape(J, 2, T * H, T * H), eye)
    w_u = w_u.reshape(J, 2 * T * H, 2 * T * H)
    w_s = jnp.einsum('jeqps,ef->jqepfs', m_out.reshape(J, 2, 4, P, T * H), eye)
    w_s = w_s.reshape(J, 8 * P, 2 * T * H)
    w_out = jnp.concatenate([w_u, w_s], axis=1)

    lv_re, lv_im = cpow(jnp.asarray([float(T * 2 ** k) for k in range(nlev)], F32))
    lv = jnp.stack([lv_re, lv_im], axis=2)
    lv = lv.reshape(nlev, 2, 2, J, 2 * P).transpose(3, 1, 2, 0, 4).reshape(J, 4, nlev, 2 * P)
    return w_in.astype(BF16), w_out.astype(BF16), lv


def _s5_body(u_ref, win_ref, wout_ref, lv_ref, y_ref, *, n_chunks, nlev):
    rows = u_ref.shape[0]
    sw = 2 * SSM_P
    gpt = LANES // SSM_H
    ppt = gpt // 2
    gran = lax.broadcasted_iota(jnp.int32, (rows, LANES), 1) // SSM_H
    pos = lax.broadcasted_iota(jnp.int32, (rows, sw), 0) & (n_chunks - 1)

    def gather(tiles, sel):
        acc = None
        for j in range(gpt):
            shift = ((j - sel) * SSM_H) % LANES
            r = tiles[j] if shift == 0 else pltpu.roll(tiles[j], shift, 1)
            acc = r if acc is None else jnp.where(gran == j, r, acc)
        return acc

    def shifted(v, shift, fwd):
        if fwd:
            return jnp.where(pos >= shift, pltpu.roll(v, shift, 0), 0.0)
        return jnp.where(pos < n_chunks - shift, pltpu.roll(v, rows - shift, 0), 0.0)

    src = [u_ref[:, t, :] for t in range(SSM_T)]
    ytiles = []
    for pi in range(ppt):
        ut = [gather(src[th * gpt:(th + 1) * gpt], 2 * pi + e) for e in range(2) for th in range(2)]
        u = jnp.concatenate(ut, axis=1).astype(BF16)
        x = jnp.dot(u, win_ref[pi], preferred_element_type=F32)
        states = []
        for d, fwd in ((0, True), (1, False)):
            s_re = x[:, (2 * d) * sw:(2 * d + 1) * sw]
            s_im = x[:, (2 * d + 1) * sw:(2 * d + 2) * sw]
            for k in range(nlev):
                shift = 2 ** k
                if shift >= n_chunks:
                    break
                a_re = lv_ref[pi, 2 * d, k:k + 1, :]
                a_im = lv_ref[pi, 2 * d + 1, k:k + 1, :]
                p_re = shifted(s_re, shift, fwd)
                p_im = shifted(s_im, shift, fwd)
                s_re, s_im = (s_re + a_re * p_re - a_im * p_im,
                              s_im + a_re * p_im + a_im * p_re)
            states.append(shifted(s_re, 1, fwd).astype(BF16))
            states.append(shifted(s_im, 1, fwd).astype(BF16))
        lhs = jnp.concatenate([u] + states, axis=1)
        y = jnp.dot(lhs, wout_ref[pi], preferred_element_type=F32)
        ytiles += [y[:, q * LANES:(q + 1) * LANES] for q in range(4)]
    for t in range(SSM_T):
        th, t8 = divmod(t, gpt)
        y_ref[:, t, :] = gather([ytiles[2 * g + th] for g in range(gpt)], t8)


def _s5_scan(u, w_in, w_out, lv, seq_len, *, rows=256):
    n = u.shape[0]
    nrow = n // SSM_T
    n_chunks = seq_len // SSM_T
    rows = max(min(rows, nrow), n_chunks)
    assert nrow % rows == 0 and rows % n_chunks == 0
    assert n_chunks & (n_chunks - 1) == 0
    nlev = lv.shape[2]
    assert 2 ** nlev >= n_chunks
    pw = 2 * SSM_TH
    ppt = LANES // SSM_H // 2
    tile = pl.BlockSpec((rows, SSM_T, LANES), lambda o, r: (r, 0, o))
    y = pl.pallas_call(
        functools.partial(_s5_body, n_chunks=n_chunks, nlev=nlev),
        grid=(SSM_PAIRS // ppt, nrow // rows),
        in_specs=[
            tile,
            pl.BlockSpec((ppt, pw, 8 * SSM_P), lambda o, r: (o, 0, 0)),
            pl.BlockSpec((ppt, pw + 8 * SSM_P, pw), lambda o, r: (o, 0, 0)),
            pl.BlockSpec((ppt, 4, nlev, 2 * SSM_P), lambda o, r: (o, 0, 0, 0)),
        ],
        out_specs=tile,
        out_shape=jax.ShapeDtypeStruct((nrow, SSM_T, SSM_W), F32),
        compiler_params=_cparams(("parallel", "parallel")),
        name="s5_scan",
    )(u.reshape(nrow, SSM_T, SSM_W), w_in, w_out, lv)
    return y.reshape(n, SSM_W)


def _s5_post_body(y_ref, u_ref, d_ref, w_ref, b_ref, gn_ref, o_ref):
    y = y_ref[...] + d_ref[...] * u_ref[...]
    g = jax.nn.gelu(y)
    z = jnp.dot(g.astype(BF16), w_ref[...], preferred_element_type=F32) + b_ref[...]
    s = g * jax.nn.sigmoid(z)
    o_ref[...] = _rmsnorm(s, gn_ref[...]).astype(BF16)


def _s5_post(y, u, d_skip, w_glu, b_glu, gain, *, tm=1024):
    n = y.shape[0]
    tm = _row_tile(n, tm)
    row = pl.BlockSpec((tm, SSM_W), lambda i: (i, 0))
    vec = _const_spec((1, SSM_W))
    return pl.pallas_call(
        _s5_post_body,
        grid=(n // tm,),
        in_specs=[row, row, vec, _const_spec((SSM_W, SSM_W)), vec, vec],
        out_specs=row,
        out_shape=jax.ShapeDtypeStruct((n, SSM_W), BF16),
        compiler_params=_cparams(("parallel",)),
        name="s5_post",
    )(y, u, d_skip, w_glu, b_glu, gain)


def _conv_body(cc_ref, cp_ref, cn_ref, w_ref, b_ref, lng_ref, lnb_ref, gn_ref, o_ref, gbuf, sbuf, cbuf,
               *, tl, nblk, rb):
    i = pl.program_id(1)

    def glu(v):
        return v[:, :CONV_W] * jax.nn.sigmoid(v[:, CONV_W:])

    gbuf[0:CONV_HALO, :] = jnp.where(i > 0, glu(cp_ref[...]), 0.0)
    gbuf[CONV_HALO:CONV_HALO + tl, :] = glu(cc_ref[...])
    gbuf[CONV_HALO + tl:2 * CONV_HALO + tl, :] = jnp.where(i < nblk - 1, glu(cn_ref[...]), 0.0)
    span = tl + 2 * CONV_HALO - SUBLANES
    for b in range(1, SUBLANES):
        sbuf[b - 1, :, :] = gbuf[b:b + span, :]
    base = CONV_HALO - CONV_K // 2

    def block(blk, carry):
        r0 = blk * rb
        acc = jnp.broadcast_to(b_ref[...], (rb // SUBLANES, SUBLANES, CONV_W))
        for k in range(CONV_K):
            a, b = divmod(base + k, SUBLANES)
            rows = pl.ds(pl.multiple_of(r0 + a * SUBLANES, SUBLANES), rb)
            src = gbuf[rows, :] if b == 0 else sbuf[b - 1, rows, :]
            acc = acc + src.reshape(rb // SUBLANES, SUBLANES, CONV_W) * w_ref[k]
        cbuf[pl.ds(pl.multiple_of(r0, rb), rb), :] = acc.reshape(rb, CONV_W)
        return carry

    lax.fori_loop(0, tl // rb, block, 0)

    nr = min(tl, LANES)
    for r0 in range(0, tl, nr):
        acc = cbuf[r0:r0 + nr, :]
        mu = jnp.mean(acc, axis=-1, keepdims=True)
        xc = acc - mu
        var = jnp.mean(xc * xc, axis=-1, keepdims=True)
        y = xc * lax.rsqrt(var + EPS) * lng_ref[...] + lnb_ref[...]
        y = jax.nn.silu(y)
        o_ref[r0:r0 + nr, :] = _rmsnorm(y, gn_ref[...]).astype(BF16)


def _conformer_conv(c, conv_w, conv_b, ln_g, ln_b, gain, batch, seq_len, *, tl=512, rb=32):
    n = c.shape[0]
    tl = _row_tile(seq_len, tl)
    rb = _row_tile(tl, rb)
    nblk = seq_len // tl
    hpt = tl // CONV_HALO
    n_hblk = n // CONV_HALO
    vec = _const_spec((1, CONV_W))
    return pl.pallas_call(
        functools.partial(_conv_body, tl=tl, nblk=nblk, rb=rb),
        grid=(batch, nblk),
        in_specs=[
            pl.BlockSpec((tl, 2 * CONV_W), lambda b, i: (b * nblk + i, 0)),
            pl.BlockSpec((CONV_HALO, 2 * CONV_W),
                         lambda b, i: (jnp.maximum((b * nblk + i) * hpt - 1, 0), 0)),
            pl.BlockSpec((CONV_HALO, 2 * CONV_W),
                         lambda b, i: (jnp.minimum((b * nblk + i + 1) * hpt, n_hblk - 1), 0)),
            _const_spec((CONV_K, SUBLANES, CONV_W)),
            vec, vec, vec, vec,
        ],
        out_specs=pl.BlockSpec((tl, CONV_W), lambda b, i: (b * nblk + i, 0)),
        out_shape=jax.ShapeDtypeStruct((n, CONV_W), BF16),
        scratch_shapes=[pltpu.VMEM((tl + 2 * CONV_HALO, CONV_W), F32),
                        pltpu.VMEM((SUBLANES - 1, tl + 2 * CONV_HALO - SUBLANES, CONV_W), F32),
                        pltpu.VMEM((tl, CONV_W), F32)],
        compiler_params=_cparams(("parallel", "parallel")),
        name="conformer_conv",
    )(c, c, c, jnp.broadcast_to(conv_w[:, None, :], (CONV_K, SUBLANES, CONV_W)), conv_b, ln_g, ln_b, gain)


def _memkv_body(m_ref, g_ref, w_ref, o_ref):
    h = _rmsnorm(m_ref[...], g_ref[...]).astype(BF16)
    o_ref[...] = jnp.dot(h, w_ref[...], preferred_element_type=F32).astype(BF16)


def _mem_kv(mem, gain, w_kv, *, tm=512):
    n, d = mem.shape
    tm = _row_tile(n, tm)
    return pl.pallas_call(
        _memkv_body,
        grid=(n // tm,),
        in_specs=[pl.BlockSpec((tm, d), lambda i: (i, 0)), _const_spec((1, d)),
                  _const_spec((d, 2 * X_W))],
        out_specs=pl.BlockSpec((tm, 2 * X_W), lambda i: (i, 0)),
        out_shape=jax.ShapeDtypeStruct((n, 2 * X_W), BF16),
        compiler_params=_cparams(("parallel",)),
        name="mem_kv",
    )(mem, gain, w_kv)


def _mixout_body(x_ref, a_ref, s_ref, c_ref, wout_ref, gx_ref, wq_ref, kv_ref, wo_ref, o_ref):
    mixed = jnp.concatenate([a_ref[...], s_ref[...], c_ref[...]], axis=1)
    x1 = x_ref[...] + jnp.dot(mixed, wout_ref[...], preferred_element_type=F32)
    h = _rmsnorm(x1, gx_ref[...]).astype(BF16)
    q = jnp.dot(h, wq_ref[...], preferred_element_type=F32).astype(BF16)
    scale = 1.0 / math.sqrt(X_HEAD_DIM)
    heads = []
    for hd in range(X_HEADS):
        ks = slice(hd * X_HEAD_DIM, (hd + 1) * X_HEAD_DIM)
        vs = slice(X_W + hd * X_HEAD_DIM, X_W + (hd + 1) * X_HEAD_DIM)
        s = lax.dot_general(q[:, ks], kv_ref[:, ks], (((1,), (1,)), ((), ())),
                            preferred_element_type=F32) * scale
        m = jnp.max(s, axis=-1, keepdims=True)
        p = jnp.exp(s - m)
        denom = jnp.sum(p, axis=-1, keepdims=True)
        o = jnp.dot(p.astype(BF16), kv_ref[:, vs], preferred_element_type=F32) / denom
        heads.append(o.astype(BF16))
    o = jnp.concatenate(heads, axis=1)
    o_ref[...] = x1 + jnp.dot(o, wo_ref[...], preferred_element_type=F32)


def _mix_out(x, a, s, c, w_out, gain_x, w_q, kv_mem, w_o, batch, seq_len, *, tm=512):
    n, d = x.shape
    tm = _row_tile(seq_len, tm)
    per_seq = seq_len // tm
    row = lambda w: pl.BlockSpec((tm, w), lambda b, i: (b * per_seq + i, 0))
    return pl.pallas_call(
        _mixout_body,
        grid=(batch, per_seq),
        in_specs=[
            row(d), row(ATTN_W), row(SSM_W), row(CONV_W),
            _const_spec((D_MIX, d)), _const_spec((1, d)), _const_spec((d, X_W)),
            pl.BlockSpec((MEM_LEN, 2 * X_W), lambda b, i: (b, 0)),
            _const_spec((X_W, d)),
        ],
        out_specs=row(d),
        out_shape=jax.ShapeDtypeStruct((n, d), F32),
        compiler_params=_cparams(("parallel", "parallel")),
        name="mix_out",
    )(x, a, s, c, w_out, gain_x, w_q, kv_mem, w_o)


def _encoder_layer(x, mem, p, rope_tabs, batch, seq_len, final_gain, final):
    vec = lambda a: a.reshape(1, -1).astype(F32)
    x = _ffn(x, vec(p['ffn1_norm']), p['ffn1_w_gate'], p['ffn1_w_up'], p['ffn1_w_down'],
             final_gain, final=False)
    q, kv, u, c = _inproj(x, vec(p['mix_norm']), p['w_in'], rope_tabs, seq_len)
    a_out = _window_attn(q, kv, p['attn_sink'].astype(F32), vec(p['attn_out_norm']), batch, seq_len)
    y = _s5_scan(u, p['s5_w_in'], p['s5_w_out'], p['s5_lv'], seq_len)
    s_out = _s5_post(y, u, vec(p['ssm_d']), p['ssm_w_glu'], vec(p['ssm_b_glu']),
                     vec(p['ssm_out_norm']))
    c_out = _conformer_conv(c, p['conv_w'].astype(F32), vec(p['conv_b']), vec(p['conv_ln_g']),
                            vec(p['conv_ln_b']), vec(p['conv_out_norm']), batch, seq_len)
    kv_mem = _mem_kv(mem, vec(p['mem_norm']), p['xattn_w_kv'])
    x = _mix_out(x, a_out, s_out, c_out, p['w_out'], vec(p['xattn_norm']), p['xattn_w_q'], kv_mem,
                 p['xattn_w_o'], batch, seq_len)
    x = _ffn(x, vec(p['ffn2_norm']), p['ffn2_w_gate'], p['ffn2_w_up'], p['ffn2_w_down'],
             final_gain, final=final)
    return x


_MATMUL_WEIGHTS = ('ffn1_w_gate', 'ffn1_w_up', 'ffn1_w_down', 'w_in', 'ssm_w_glu', 'w_out',
                   'xattn_w_q', 'xattn_w_kv', 'xattn_w_o', 'ffn2_w_gate', 'ffn2_w_up', 'ffn2_w_down')


def _trunk(groups, stacked, final_norm):
    depth = stacked['w_in'].shape[0]
    final_gain = final_norm.reshape(1, -1).astype(F32)
    max_len = max(x.shape[1] for x, _ in groups)
    nlev = max(1, (max_len // SSM_T - 1).bit_length())
    state = []
    for x, mem in groups:
        b, l, d = x.shape
        state.append([x.reshape(b * l, d), mem.reshape(-1, d), _rope_tables(l), b, l])
    for layer in range(depth):
        p = {k: v[layer] for k, v in stacked.items()}
        for k in _MATMUL_WEIGHTS:
            p[k] = p[k].astype(BF16)
        p['s5_w_in'], p['s5_w_out'], p['s5_lv'] = _s5_operators(
            p['ssm_lambda_re'], p['ssm_lambda_im'], p['ssm_log_dt'], p['ssm_b_re'], p['ssm_b_im'],
            p['ssm_c_re'], p['ssm_c_im'], nlev)
        for st in state:
            st[0] = _encoder_layer(st[0], st[1], p, st[2], st[3], st[4], final_gain,
                                   final=(layer == depth - 1))
    return tuple(st[0].reshape(st[3], st[4], -1) for st in state)


def kernel(x_prompt, x_sample, mem_prompt, mem_sample, ffn1_norm, ffn1_w_gate, ffn1_w_up, ffn1_w_down, mix_norm, w_in, attn_sink, ssm_lambda_re, ssm_lambda_im, ssm_log_dt, ssm_b_re, ssm_b_im, ssm_c_re, ssm_c_im, ssm_d, ssm_w_glu, ssm_b_glu, conv_w, conv_b, conv_ln_g, conv_ln_b, attn_out_norm, ssm_out_norm, conv_out_norm, w_out, xattn_norm, mem_norm, xattn_w_q, xattn_w_kv, xattn_w_o, ffn2_norm, ffn2_w_gate, ffn2_w_up, ffn2_w_down, final_norm):
    stacked = {
        'ffn1_norm': ffn1_norm, 'ffn1_w_gate': ffn1_w_gate, 'ffn1_w_up': ffn1_w_up, 'ffn1_w_down': ffn1_w_down,
        'mix_norm': mix_norm, 'w_in': w_in, 'attn_sink': attn_sink,
        'ssm_lambda_re': ssm_lambda_re, 'ssm_lambda_im': ssm_lambda_im, 'ssm_log_dt': ssm_log_dt,
        'ssm_b_re': ssm_b_re, 'ssm_b_im': ssm_b_im, 'ssm_c_re': ssm_c_re, 'ssm_c_im': ssm_c_im,
        'ssm_d': ssm_d, 'ssm_w_glu': ssm_w_glu, 'ssm_b_glu': ssm_b_glu,
        'conv_w': conv_w, 'conv_b': conv_b, 'conv_ln_g': conv_ln_g, 'conv_ln_b': conv_ln_b,
        'attn_out_norm': attn_out_norm, 'ssm_out_norm': ssm_out_norm, 'conv_out_norm': conv_out_norm,
        'w_out': w_out,
        'xattn_norm': xattn_norm, 'mem_norm': mem_norm,
        'xattn_w_q': xattn_w_q, 'xattn_w_kv': xattn_w_kv, 'xattn_w_o': xattn_w_o,
        'ffn2_norm': ffn2_norm, 'ffn2_w_gate': ffn2_w_gate, 'ffn2_w_up': ffn2_w_up, 'ffn2_w_down': ffn2_w_down,
    }
    y_prompt, y_sample = _trunk([(x_prompt, mem_prompt), (x_sample, mem_sample)], stacked, final_norm)
    return (y_prompt, y_sample)
```

```python
import functools
import math

import jax
import jax.numpy as jnp
from jax import lax
from jax.experimental import pallas as pl
from jax.experimental.pallas import tpu as pltpu

F32 = jnp.float32
BF16 = jnp.bfloat16

D_MODEL = 2048
DEPTH = 4
HEAD_DIM = 128
N_Q_HEADS = 8
N_KV_HEADS = 2
Q_PER_KV = N_Q_HEADS // N_KV_HEADS
ATTN_W = N_Q_HEADS * HEAD_DIM
KV_W = N_KV_HEADS * HEAD_DIM
WINDOW = 128
ROPE_DIM = HEAD_DIM // 4
ROPE_THETA = 500000.0
SSM_W = 512
SSM_H = 16
SSM_G = SSM_W // SSM_H
SSM_P = 64
CONV_W = 512
CONV_K = 31
D_MIX = ATTN_W + SSM_W + CONV_W
IN_COLS = ATTN_W + 2 * KV_W + SSM_W + 2 * CONV_W
MEM_LEN = 256
X_HEADS = 4
X_HEAD_DIM = 128
X_W = X_HEADS * X_HEAD_DIM
D_FF = 5632
EPS = 1e-6
NEG = -1e30

O_Q = 0
O_KV = ATTN_W
O_U = ATTN_W + 2 * KV_W
O_C = O_U + SSM_W

SSM_T = 16
SSM_PAIRS = SSM_G // 2
SSM_TH = SSM_T * SSM_H

V7X_VMEM_BYTES = 64 * 1024 * 1024
VMEM_LIMIT = V7X_VMEM_BYTES - 8 * 1024 * 1024
LANES = 128
SUBLANES = 8
CONV_HALO = 2 * SUBLANES


def _cparams(sem):
    return pltpu.CompilerParams(dimension_semantics=sem, vmem_limit_bytes=VMEM_LIMIT)


def _const_spec(shape):
    nd = len(shape)
    return pl.BlockSpec(shape, lambda *_: (0,) * nd)


def _layer_spec(layer, shape, imap=None):
    if imap is None:
        imap = lambda *_: (0,) * len(shape)
    return pl.BlockSpec((None,) + tuple(shape), lambda *idx: (layer,) + tuple(imap(*idx)))


def _rmsnorm(xf, g):
    ms = jnp.mean(xf * xf, axis=-1, keepdims=True)
    return xf * lax.rsqrt(ms + EPS) * g


def _row_tile(n, pref):
    t = min(n, pref)
    assert n % t == 0, (n, t)
    return t


def _ffn_body(x_ref, g_ref, wg_ref, wu_ref, wd_ref, fg_ref, o_ref, h_ref, *, n_ff, final):
    j = pl.program_id(1)

    @pl.when(j == 0)
    def _():
        tm = o_ref.shape[0]
        rc = min(tm, 2 * LANES)
        for r0 in range(0, tm, rc):
            h_ref[r0:r0 + rc, :] = _rmsnorm(x_ref[r0:r0 + rc, :], g_ref[...]).astype(BF16)
        o_ref[...] = jnp.zeros_like(o_ref)

    h = h_ref[...]
    g = jnp.dot(h, wg_ref[...], preferred_element_type=F32)
    u = jnp.dot(h, wu_ref[...], preferred_element_type=F32)
    a = (jax.nn.silu(g) * u).astype(BF16)
    o_ref[...] += jnp.dot(a, wd_ref[...], preferred_element_type=F32)

    @pl.when(j == n_ff - 1)
    def _():
        tm = o_ref.shape[0]
        rc = min(tm, LANES)

        def chunk(r, carry):
            rows = pl.ds(pl.multiple_of(r * rc, rc), rc)
            y = x_ref[rows, :] + 0.5 * o_ref[rows, :]
            if final:
                y = _rmsnorm(y, fg_ref[...])
            o_ref[rows, :] = y
            return carry

        lax.fori_loop(0, tm // rc, chunk, 0)


def _ffn(x, gain, wg, wu, wd, final_gain, layer, *, final, tm=1024, tf=512):
    n, d = x.shape
    dff = wg.shape[-1]
    tm = _row_tile(n, tm)
    tf = _row_tile(dff, tf)
    n_ff = dff // tf
    return pl.pallas_call(
        functools.partial(_ffn_body, n_ff=n_ff, final=final),
        grid=(n // tm, n_ff),
        in_specs=[
            pl.BlockSpec((tm, d), lambda i, j: (i, 0)),
            _const_spec((1, d)),
            _layer_spec(layer, (d, tf), lambda i, j: (0, j)),
            _layer_spec(layer, (d, tf), lambda i, j: (0, j)),
            _layer_spec(layer, (tf, d), lambda i, j: (j, 0)),
            _const_spec((1, d)),
        ],
        out_specs=pl.BlockSpec((tm, d), lambda i, j: (i, 0)),
        out_shape=jax.ShapeDtypeStruct((n, d), F32),
        scratch_shapes=[pltpu.VMEM((tm, d), BF16)],
        compiler_params=_cparams(("parallel", "arbitrary")),
        name="ffn",
    )(x, gain, wg, wu, wd, final_gain)


def _rope_tables(seq_len):
    half = ROPE_DIM // 2
    inv_freq = ROPE_THETA ** (-(2.0 * jnp.arange(half, dtype=F32)) / ROPE_DIM)
    ang = jnp.arange(seq_len, dtype=jnp.int32).astype(F32)[:, None] * inv_freq[None, :]
    cos = jnp.cos(ang)
    sin = jnp.sin(ang)
    zeros = jnp.zeros((seq_len, HEAD_DIM - ROPE_DIM), F32)
    zh = jnp.zeros((seq_len, half), F32)
    cos_t = jnp.concatenate([cos, cos, jnp.ones_like(zeros)], axis=1)
    s1_t = jnp.concatenate([-sin, zh, zeros], axis=1)
    s2_t = jnp.concatenate([zh, sin, zeros], axis=1)
    return cos_t, s1_t, s2_t


def _inproj_body(x_ref, g_ref, w_ref, cos_ref, s1_ref, s2_ref, q_ref, kv_ref, u_ref, c_ref):
    h = _rmsnorm(x_ref[...], g_ref[...]).astype(BF16)
    cos_t, s1_t, s2_t = cos_ref[...], s1_ref[...], s2_ref[...]
    half = ROPE_DIM // 2

    def rope(z):
        return (z * cos_t + pltpu.roll(z, HEAD_DIM - half, 1) * s1_t
                + pltpu.roll(z, half, 1) * s2_t)

    zq = jnp.dot(h, w_ref[:, O_Q:O_KV], preferred_element_type=F32)
    for hd in range(N_Q_HEADS):
        sl = slice(hd * HEAD_DIM, (hd + 1) * HEAD_DIM)
        q_ref[:, sl] = rope(zq[:, sl]).astype(BF16)
    zkv = jnp.dot(h, w_ref[:, O_KV:O_U], preferred_element_type=F32)
    for hd in range(N_KV_HEADS):
        sl = slice(hd * HEAD_DIM, (hd + 1) * HEAD_DIM)
        kv_ref[:, sl] = rope(zkv[:, sl]).astype(BF16)
    kv_ref[:, KV_W:] = zkv[:, KV_W:].astype(BF16)
    u_ref[...] = jnp.dot(h, w_ref[:, O_U:O_C], preferred_element_type=F32)
    c_ref[...] = jnp.dot(h, w_ref[:, O_C:], preferred_element_type=F32)


def _inproj(x, gain, w_in, layer, rope_tabs, seq_len, *, tm=512):
    n, d = x.shape
    tm = _row_tile(seq_len, tm)
    per_seq = seq_len // tm
    tab_spec = pl.BlockSpec((tm, HEAD_DIM), lambda i: (i % per_seq, 0))
    row = lambda w: pl.BlockSpec((tm, w), lambda i: (i, 0))
    return pl.pallas_call(
        _inproj_body,
        grid=(n // tm,),
        in_specs=[row(d), _const_spec((1, d)), _layer_spec(layer, (d, IN_COLS)), tab_spec, tab_spec, tab_spec],
        out_specs=[row(ATTN_W), row(2 * KV_W), row(SSM_W), row(2 * CONV_W)],
        out_shape=[
            jax.ShapeDtypeStruct((n, ATTN_W), BF16),
            jax.ShapeDtypeStruct((n, 2 * KV_W), BF16),
            jax.ShapeDtypeStruct((n, SSM_W), F32),
            jax.ShapeDtypeStruct((n, 2 * CONV_W), F32),
        ],
        compiler_params=_cparams(("parallel",)),
        name="inproj",
    )(x, gain, w_in, *rope_tabs)


def _attn_body(sink_ref, q_ref, kvc_ref, kvp_ref, kvn_ref, gn_ref, o_ref, kvbuf, acc_ref,
               *, tq, seq_len):
    i = pl.program_id(1)
    w = WINDOW
    nk = 3 * w
    kvbuf[0:w, :] = kvp_ref[...]
    kvbuf[w:w + tq, :] = kvc_ref[...]
    kvbuf[w + tq:, :] = kvn_ref[...]
    t = lax.broadcasted_iota(jnp.int32, (w, nk), 0)
    c = lax.broadcasted_iota(jnp.int32, (w, nk), 1)
    scale = 1.0 / math.sqrt(HEAD_DIM)
    for sb in range(tq // w):
        p0 = i * tq + sb * w
        lo = jnp.maximum(t, w - p0)
        hi = jnp.minimum(t + 2 * w, seq_len - p0 + w - 1)
        bias = jnp.where(c >= lo, jnp.where(c <= hi, 0.0, NEG), NEG)
        rows = slice(sb * w, (sb + 1) * w)
        krows = slice(sb * w, sb * w + nk)
        for hk in range(N_KV_HEADS):
            k = kvbuf[krows, hk * HEAD_DIM:(hk + 1) * HEAD_DIM]
            v = kvbuf[krows, KV_W + hk * HEAD_DIM:KV_W + (hk + 1) * HEAD_DIM]
            heads = range(hk * Q_PER_KV, (hk + 1) * Q_PER_KV)
            qs = jnp.concatenate([q_ref[rows, h * HEAD_DIM:(h + 1) * HEAD_DIM] for h in heads], axis=0)
            s = lax.dot_general(qs, k, (((1,), (1,)), ((), ())), preferred_element_type=F32) * scale
            ps, dens = [], []
            for g, h in enumerate(heads):
                sg = s[g * w:(g + 1) * w] + bias
                sk = sink_ref[h]
                m = jnp.maximum(jnp.max(sg, axis=-1, keepdims=True), sk)
                p = jnp.exp(sg - m)
                dens.append(jnp.sum(p, axis=-1, keepdims=True) + jnp.exp(sk - m))
                ps.append(p.astype(BF16))
            o = jnp.dot(jnp.concatenate(ps, axis=0), v, preferred_element_type=F32)
            for g, h in enumerate(heads):
                acc_ref[rows, h * HEAD_DIM:(h + 1) * HEAD_DIM] = o[g * w:(g + 1) * w] / dens[g]
    o_ref[...] = _rmsnorm(acc_ref[...], gn_ref[...]).astype(BF16)


def _window_attn(q, kv, sink, gain, batch, seq_len, *, tq=512):
    n = q.shape[0]
    tq = _row_tile(seq_len, tq)
    nq = seq_len // tq
    wpt = tq // WINDOW
    n_wblk = n // WINDOW
    return pl.pallas_call(
        functools.partial(_attn_body, tq=tq, seq_len=seq_len),
        grid=(batch, nq),
        in_specs=[
            pl.BlockSpec(memory_space=pltpu.SMEM),
            pl.BlockSpec((tq, ATTN_W), lambda b, i: (b * nq + i, 0)),
            pl.BlockSpec((tq, 2 * KV_W), lambda b, i: (b * nq + i, 0)),
            pl.BlockSpec((WINDOW, 2 * KV_W),
                         lambda b, i: (jnp.maximum((b * nq + i) * wpt - 1, 0), 0)),
            pl.BlockSpec((WINDOW, 2 * KV_W),
                         lambda b, i: (jnp.minimum((b * nq + i + 1) * wpt, n_wblk - 1), 0)),
            _const_spec((1, ATTN_W)),
        ],
        out_specs=pl.BlockSpec((tq, ATTN_W), lambda b, i: (b * nq + i, 0)),
        out_shape=jax.ShapeDtypeStruct((n, ATTN_W), BF16),
        scratch_shapes=[pltpu.VMEM((tq + 2 * WINDOW, 2 * KV_W), BF16),
                        pltpu.VMEM((tq, ATTN_W), F32)],
        compiler_params=_cparams(("parallel", "parallel")),
        name="window_attn",
    )(sink, q, kv, kv, kv, gain)


def _s5_operators(lam_re, lam_im, log_dt, b_re, b_im, c_re, c_im, nlev):
    hp = lax.Precision.HIGHEST
    T, H, P, G = SSM_T, SSM_H, SSM_P, SSM_G
    lr = jnp.minimum(lam_re.astype(F32), -1e-4)
    li = lam_im.astype(F32)
    dt = jnp.exp(log_dt.astype(F32))[..., None]
    lg = lr * dt
    th = li * dt

    def cpow(tau):
        tau = tau[:, None, None, None]
        mag = jnp.exp(lg * tau)
        return mag * jnp.cos(th * tau), mag * jnp.sin(th * tau)

    ab_re, ab_im = (v[0] for v in cpow(jnp.ones((1,), F32)))
    den = lr * lr + li * li
    n_re = ab_re - 1.0
    n_im = ab_im
    f_re = (n_re * lr + n_im * li) / den
    f_im = (n_im * lr - n_re * li) / den
    br = b_re.astype(F32)
    bi = b_im.astype(F32)
    bb_re = f_re[..., None] * br - f_im[..., None] * bi
    bb_im = f_re[..., None] * bi + f_im[..., None] * br
    cr = c_re.astype(F32)
    ci = c_im.astype(F32)

    pw_re, pw_im = cpow(jnp.arange(T + 1, dtype=F32))
    w_re = pw_re[..., None] * bb_re - pw_im[..., None] * bb_im
    w_im = pw_re[..., None] * bb_im + pw_im[..., None] * bb_re
    kmat = (jnp.einsum('dgop,tdgpi->tdgoi', cr, w_re, precision=hp)
            - jnp.einsum('dgop,tdgpi->tdgoi', ci, w_im, precision=hp))

    s_idx = jnp.arange(T)[:, None]
    t_idx = jnp.arange(T)[None, :]
    kf = kmat[jnp.clip(t_idx - s_idx, 0, T), 0]
    kb = kmat[jnp.clip(s_idx - t_idx, 0, T), 1]
    tri_f = (t_idx >= s_idx)[:, :, None, None, None]
    tri_b = (s_idx >= t_idx)[:, :, None, None, None]
    m_intra = jnp.where(tri_f, kf, 0.0) + jnp.where(tri_b, kb, 0.0)
    m_intra = m_intra.transpose(2, 0, 4, 1, 3).reshape(G, T * H, T * H)

    def to_in(wd):
        return wd.transpose(1, 0, 3, 2).reshape(G, T * H, P)

    rev = jnp.arange(T - 1, -1, -1)
    fwd = jnp.arange(T)
    m_in = jnp.stack([to_in(w_re[rev, 0]), to_in(w_im[rev, 0]),
                      to_in(w_re[fwd, 1]), to_in(w_im[fwd, 1])], axis=2)

    def to_out(pr, pi, d):
        wr = cr[d][None] * pr[:, :, None, :] - ci[d][None] * pi[:, :, None, :]
        wi = cr[d][None] * pi[:, :, None, :] + ci[d][None] * pr[:, :, None, :]
        tr = lambda a: a.transpose(1, 3, 0, 2).reshape(G, P, T * H)
        return tr(wr), tr(-wi)

    of_re, of_im = to_out(pw_re[1:, 0], pw_im[1:, 0], 0)
    ob_re, ob_im = to_out(pw_re[T - fwd, 1], pw_im[T - fwd, 1], 1)
    m_out = jnp.stack([of_re, of_im, ob_re, ob_im], axis=1)

    eye = jnp.eye(2, dtype=F32)
    J = SSM_PAIRS
    w_in = jnp.einsum('jerqp,ef->jerqfp', m_in.reshape(J, 2, T * H, 4, P), eye)
    w_in = w_in.reshape(J, 2 * T * H, 8 * P)
    w_u = jnp.einsum('jers,ef->jerfs', m_intra.reshape(J, 2, T * H, T * H), eye)
    w_u = w_u.reshape(J, 2 * T * H, 2 * T * H)
    w_s = jnp.einsum('jeqps,ef->jqepfs', m_out.reshape(J, 2, 4, P, T * H), eye)
    w_s = w_s.reshape(J, 8 * P, 2 * T * H)
    w_out = jnp.concatenate([w_u, w_s], axis=1)

    lv_re, lv_im = cpow(jnp.asarray([float(T * 2 ** k) for k in range(nlev)], F32))
    lv = jnp.stack([lv_re, lv_im], axis=2)
    lv = lv.reshape(nlev, 2, 2, J, 2 * P).transpose(3, 1, 2, 0, 4).reshape(J, 4, nlev, 2 * P)
    return w_in.astype(BF16), w_out.astype(BF16), lv


def _s5_body(u_ref, win_ref, wout_ref, lv_ref, y_ref, *, n_chunks, nlev):
    rows = u_ref.shape[0] // SSM_T
    sw = 2 * SSM_P
    gpt = LANES // SSM_H
    ppt = gpt // 2
    gran = lax.broadcasted_iota(jnp.int32, (rows, LANES), 1) // SSM_H
    pos = lax.broadcasted_iota(jnp.int32, (rows, sw), 0) & (n_chunks - 1)

    def granule_transpose(tiles):
        tiles = list(tiles)
        step = gpt // 2
        while step >= 1:
            low = (gran & step) == 0
            nxt = list(tiles)
            for a in range(gpt):
                if a & step == 0:
                    lo, hi = tiles[a], tiles[a + step]
                    nxt[a] = jnp.where(low, lo, pltpu.roll(hi, step * SSM_H, 1))
                    nxt[a + step] = jnp.where(low, pltpu.roll(lo, LANES - step * SSM_H, 1), hi)
            tiles = nxt
            step //= 2
        return tiles

    def shifted(v, shift, fwd):
        if shift % SUBLANES == 0:
            z = jnp.zeros((shift, v.shape[1]), v.dtype)
            moved = (jnp.concatenate([z, v[:rows - shift]], axis=0) if fwd
                     else jnp.concatenate([v[shift:], z], axis=0))
            if rows == n_chunks:
                return moved
        else:
            moved = pltpu.roll(v, shift if fwd else rows - shift, 0)
        keep = (pos >= shift) if fwd else (pos < n_chunks - shift)
        return jnp.where(keep, moved, 0.0)

    src = [u_ref[pl.ds(t, rows, stride=SSM_T), :] for t in range(SSM_T)]
    by_group = [granule_transpose(src[th * gpt:(th + 1) * gpt]) for th in range(2)]
    ytiles = [[None] * gpt for _ in range(2)]
    for pi in range(ppt):
        u32 = jnp.concatenate([by_group[th][2 * pi + e] for e in range(2) for th in range(2)], axis=1)
        u = u32.astype(BF16)
        x = jnp.dot(u, win_ref[pi], preferred_element_type=F32)
        states = []
        for d, fwd in ((0, True), (1, False)):
            s_re = x[:, (2 * d) * sw:(2 * d + 1) * sw]
            s_im = x[:, (2 * d + 1) * sw:(2 * d + 2) * sw]
            for k in range(nlev):
                shift = 2 ** k
                if shift >= n_chunks:
                    break
                a_re = lv_ref[pi, 2 * d, k:k + 1, :]
                a_im = lv_ref[pi, 2 * d + 1, k:k + 1, :]
                p_re = shifted(s_re, shift, fwd)
                p_im = shifted(s_im, shift, fwd)
                s_re, s_im = (s_re + a_re * p_re - a_im * p_im,
                              s_im + a_re * p_im + a_im * p_re)
            states.append(shifted(s_re, 1, fwd))
            states.append(shifted(s_im, 1, fwd))
        lhs = jnp.concatenate([u32] + states, axis=1).astype(BF16)
        y = jnp.dot(lhs, wout_ref[pi], preferred_element_type=F32)
        for e in range(2):
            for th in range(2):
                ytiles[th][2 * pi + e] = y[:, (2 * e + th) * LANES:(2 * e + th + 1) * LANES]
    for th in range(2):
        for t8, tile in enumerate(granule_transpose(ytiles[th])):
            y_ref[pl.ds(th * gpt + t8, rows, stride=SSM_T), :] = tile


def _s5_scan(u, w_in, w_out, lv, layer, seq_len, *, rows=128):
    n = u.shape[0]
    nrow = n // SSM_T
    n_chunks = seq_len // SSM_T
    rows = max(min(rows, nrow), n_chunks)
    assert nrow % rows == 0 and rows % n_chunks == 0
    assert n_chunks & (n_chunks - 1) == 0
    nlev = lv.shape[-2]
    assert 2 ** nlev >= n_chunks
    pw = 2 * SSM_TH
    ppt = LANES // SSM_H // 2
    tile = pl.BlockSpec((rows * SSM_T, LANES), lambda o, r: (r, o))
    return pl.pallas_call(
        functools.partial(_s5_body, n_chunks=n_chunks, nlev=nlev),
        grid=(SSM_PAIRS // ppt, nrow // rows),
        in_specs=[
            tile,
            _layer_spec(layer, (ppt, pw, 8 * SSM_P), lambda o, r: (o, 0, 0)),
            _layer_spec(layer, (ppt, pw + 8 * SSM_P, pw), lambda o, r: (o, 0, 0)),
            _layer_spec(layer, (ppt, 4, nlev, 2 * SSM_P), lambda o, r: (o, 0, 0, 0)),
        ],
        out_specs=tile,
        out_shape=jax.ShapeDtypeStruct((n, SSM_W), F32),
        compiler_params=_cparams(("parallel", "parallel")),
        name="s5_scan",
    )(u, w_in, w_out, lv)


def _s5_post_body(y_ref, u_ref, d_ref, w_ref, b_ref, gn_ref, o_ref):
    y = y_ref[...] + d_ref[...] * u_ref[...]
    g = jax.nn.gelu(y)
    z = jnp.dot(g.astype(BF16), w_ref[...], preferred_element_type=F32) + b_ref[...]
    s = g * jax.nn.sigmoid(z)
    o_ref[...] = _rmsnorm(s, gn_ref[...]).astype(BF16)


def _s5_post(y, u, d_skip, w_glu, layer, b_glu, gain, *, tm=1024):
    n = y.shape[0]
    tm = _row_tile(n, tm)
    row = pl.BlockSpec((tm, SSM_W), lambda i: (i, 0))
    vec = _const_spec((1, SSM_W))
    return pl.pallas_call(
        _s5_post_body,
        grid=(n // tm,),
        in_specs=[row, row, vec, _layer_spec(layer, (SSM_W, SSM_W)), vec, vec],
        out_specs=row,
        out_shape=jax.ShapeDtypeStruct((n, SSM_W), BF16),
        compiler_params=_cparams(("parallel",)),
        name="s5_post",
    )(y, u, d_skip, w_glu, b_glu, gain)


def _conv_body(cc_ref, cp_ref, cn_ref, w_ref, b_ref, lng_ref, lnb_ref, gn_ref, o_ref, gbuf, sbuf, cbuf,
               *, tl, nblk, rb):
    i = pl.program_id(1)

    def glu(v):
        return v[:, :CONV_W] * jax.nn.sigmoid(v[:, CONV_W:])

    gbuf[0:CONV_HALO, :] = jnp.where(i > 0, glu(cp_ref[...]), 0.0)
    gbuf[CONV_HALO:CONV_HALO + tl, :] = glu(cc_ref[...])
    gbuf[CONV_HALO + tl:2 * CONV_HALO + tl, :] = jnp.where(i < nblk - 1, glu(cn_ref[...]), 0.0)
    span = tl + 2 * CONV_HALO - SUBLANES
    for b in range(1, SUBLANES):
        sbuf[b - 1, :, :] = gbuf[b:b + span, :]
    base = CONV_HALO - CONV_K // 2

    def block(blk, carry):
        r0 = blk * rb
        acc = jnp.broadcast_to(b_ref[...], (rb // SUBLANES, SUBLANES, CONV_W))
        for k in range(CONV_K):
            a, b = divmod(base + k, SUBLANES)
            rows = pl.ds(pl.multiple_of(r0 + a * SUBLANES, SUBLANES), rb)
            src = gbuf[rows, :] if b == 0 else sbuf[b - 1, rows, :]
            acc = acc + src.reshape(rb // SUBLANES, SUBLANES, CONV_W) * w_ref[k]
        cbuf[pl.ds(pl.multiple_of(r0, rb), rb), :] = acc.reshape(rb, CONV_W)
        return carry

    lax.fori_loop(0, tl // rb, block, 0)

    nr = min(tl, LANES)
    for r0 in range(0, tl, nr):
        acc = cbuf[r0:r0 + nr, :]
        mu = jnp.mean(acc, axis=-1, keepdims=True)
        xc = acc - mu
        var = jnp.mean(xc * xc, axis=-1, keepdims=True)
        y = xc * lax.rsqrt(var + EPS) * lng_ref[...] + lnb_ref[...]
        y = jax.nn.silu(y)
        o_ref[r0:r0 + nr, :] = _rmsnorm(y, gn_ref[...]).astype(BF16)


def _conformer_conv(c, conv_w, conv_b, ln_g, ln_b, gain, batch, seq_len, *, tl=512, rb=32):
    n = c.shape[0]
    tl = _row_tile(seq_len, tl)
    rb = _row_tile(tl, rb)
    nblk = seq_len // tl
    hpt = tl // CONV_HALO
    n_hblk = n // CONV_HALO
    vec = _const_spec((1, CONV_W))
    return pl.pallas_call(
        functools.partial(_conv_body, tl=tl, nblk=nblk, rb=rb),
        grid=(batch, nblk),
        in_specs=[
            pl.BlockSpec((tl, 2 * CONV_W), lambda b, i: (b * nblk + i, 0)),
            pl.BlockSpec((CONV_HALO, 2 * CONV_W),
                         lambda b, i: (jnp.maximum((b * nblk + i) * hpt - 1, 0), 0)),
            pl.BlockSpec((CONV_HALO, 2 * CONV_W),
                         lambda b, i: (jnp.minimum((b * nblk + i + 1) * hpt, n_hblk - 1), 0)),
            _const_spec((CONV_K, SUBLANES, CONV_W)),
            vec, vec, vec, vec,
        ],
        out_specs=pl.BlockSpec((tl, CONV_W), lambda b, i: (b * nblk + i, 0)),
        out_shape=jax.ShapeDtypeStruct((n, CONV_W), BF16),
        scratch_shapes=[pltpu.VMEM((tl + 2 * CONV_HALO, CONV_W), F32),
                        pltpu.VMEM((SUBLANES - 1, tl + 2 * CONV_HALO - SUBLANES, CONV_W), F32),
                        pltpu.VMEM((tl, CONV_W), F32)],
        compiler_params=_cparams(("parallel", "parallel")),
        name="conformer_conv",
    )(c, c, c, jnp.broadcast_to(conv_w[:, None, :], (CONV_K, SUBLANES, CONV_W)), conv_b, ln_g, ln_b, gain)


def _memkv_body(m_ref, g_ref, w_ref, o_ref):
    h = _rmsnorm(m_ref[...], g_ref[...]).astype(BF16)
    o_ref[...] = jnp.dot(h, w_ref[...], preferred_element_type=F32).astype(BF16)


def _mem_kv(mem, gain, w_kv, layer, *, tm=512):
    n, d = mem.shape
    tm = _row_tile(n, tm)
    return pl.pallas_call(
        _memkv_body,
        grid=(n // tm,),
        in_specs=[pl.BlockSpec((tm, d), lambda i: (i, 0)), _const_spec((1, d)),
                  _layer_spec(layer, (d, 2 * X_W))],
        out_specs=pl.BlockSpec((tm, 2 * X_W), lambda i: (i, 0)),
        out_shape=jax.ShapeDtypeStruct((n, 2 * X_W), BF16),
        compiler_params=_cparams(("parallel",)),
        name="mem_kv",
    )(mem, gain, w_kv)


def _mixout_body(x_ref, a_ref, s_ref, c_ref, wout_ref, gx_ref, wq_ref, kv_ref, wo_ref, o_ref):
    mixed = jnp.concatenate([a_ref[...], s_ref[...], c_ref[...]], axis=1)
    x1 = x_ref[...] + jnp.dot(mixed, wout_ref[...], preferred_element_type=F32)
    h = _rmsnorm(x1, gx_ref[...]).astype(BF16)
    q = jnp.dot(h, wq_ref[...], preferred_element_type=F32).astype(BF16)
    scale = 1.0 / math.sqrt(X_HEAD_DIM)
    heads = []
    for hd in range(X_HEADS):
        ks = slice(hd * X_HEAD_DIM, (hd + 1) * X_HEAD_DIM)
        vs = slice(X_W + hd * X_HEAD_DIM, X_W + (hd + 1) * X_HEAD_DIM)
        s = lax.dot_general(q[:, ks], kv_ref[:, ks], (((1,), (1,)), ((), ())),
                            preferred_element_type=F32) * scale
        m = jnp.max(s, axis=-1, keepdims=True)
        p = jnp.exp(s - m)
        denom = jnp.sum(p, axis=-1, keepdims=True)
        o = jnp.dot(p.astype(BF16), kv_ref[:, vs], preferred_element_type=F32) / denom
        heads.append(o.astype(BF16))
    o = jnp.concatenate(heads, axis=1)
    o_ref[...] = x1 + jnp.dot(o, wo_ref[...], preferred_element_type=F32)


def _mix_out(x, a, s, c, w_out, gain_x, w_q, kv_mem, w_o, layer, batch, seq_len, *, tm=512):
    n, d = x.shape
    tm = _row_tile(seq_len, tm)
    per_seq = seq_len // tm
    row = lambda w: pl.BlockSpec((tm, w), lambda b, i: (b * per_seq + i, 0))
    return pl.pallas_call(
        _mixout_body,
        grid=(batch, per_seq),
        in_specs=[
            row(d), row(ATTN_W), row(SSM_W), row(CONV_W),
            _layer_spec(layer, (D_MIX, d)), _const_spec((1, d)), _layer_spec(layer, (d, X_W)),
            pl.BlockSpec((MEM_LEN, 2 * X_W), lambda b, i: (b, 0)),
            _layer_spec(layer, (X_W, d)),
        ],
        out_specs=row(d),
        out_shape=jax.ShapeDtypeStruct((n, d), F32),
        compiler_params=_cparams(("parallel", "parallel")),
        name="mix_out",
    )(x, a, s, c, w_out, gain_x, w_q, kv_mem, w_o)


def _encoder_layer(x, mem, p, w, layer, rope_tabs, batch, seq_len, final_gain, final):
    vec = lambda a: a.reshape(1, -1).astype(F32)
    x = _ffn(x, vec(p['ffn1_norm']), w['ffn1_w_gate'], w['ffn1_w_up'], w['ffn1_w_down'],
             final_gain, layer, final=False)
    q, kv, u, c = _inproj(x, vec(p['mix_norm']), w['w_in'], layer, rope_tabs, seq_len)
    a_out = _window_attn(q, kv, p['attn_sink'].astype(F32), vec(p['attn_out_norm']), batch, seq_len)
    y = _s5_scan(u, w['s5_w_in'], w['s5_w_out'], w['s5_lv'], layer, seq_len)
    s_out = _s5_post(y, u, vec(p['ssm_d']), w['ssm_w_glu'], layer, vec(p['ssm_b_glu']),
                     vec(p['ssm_out_norm']))
    c_out = _conformer_conv(c, p['conv_w'].astype(F32), vec(p['conv_b']), vec(p['conv_ln_g']),
                            vec(p['conv_ln_b']), vec(p['conv_out_norm']), batch, seq_len)
    kv_mem = _mem_kv(mem, vec(p['mem_norm']), w['xattn_w_kv'], layer)
    x = _mix_out(x, a_out, s_out, c_out, w['w_out'], vec(p['xattn_norm']), w['xattn_w_q'], kv_mem,
                 w['xattn_w_o'], layer, batch, seq_len)
    x = _ffn(x, vec(p['ffn2_norm']), w['ffn2_w_gate'], w['ffn2_w_up'], w['ffn2_w_down'],
             final_gain, layer, final=final)
    return x


_MATMUL_WEIGHTS = ('ffn1_w_gate', 'ffn1_w_up', 'ffn1_w_down', 'w_in', 'ssm_w_glu', 'w_out',
                   'xattn_w_q', 'xattn_w_kv', 'xattn_w_o', 'ffn2_w_gate', 'ffn2_w_up', 'ffn2_w_down')


def _trunk(groups, stacked, final_norm):
    depth = stacked['w_in'].shape[0]
    final_gain = final_norm.reshape(1, -1).astype(F32)
    max_len = max(x.shape[1] for x, _ in groups)
    nlev = max(1, (max_len // SSM_T - 1).bit_length())
    state = []
    for x, mem in groups:
        b, l, d = x.shape
        state.append([x.reshape(b * l, d), mem.reshape(-1, d), _rope_tables(l), b, l])
    w = {k: stacked[k].astype(BF16) for k in _MATMUL_WEIGHTS}
    w['s5_w_in'], w['s5_w_out'], w['s5_lv'] = jax.vmap(functools.partial(_s5_operators, nlev=nlev))(
        stacked['ssm_lambda_re'], stacked['ssm_lambda_im'], stacked['ssm_log_dt'], stacked['ssm_b_re'],
        stacked['ssm_b_im'], stacked['ssm_c_re'], stacked['ssm_c_im'])
    small = [k for k in stacked if k not in _MATMUL_WEIGHTS]
    for layer in range(depth):
        p = {k: stacked[k][layer] for k in small}
        for st in state:
            st[0] = _encoder_layer(st[0], st[1], p, w, layer, st[2], st[3], st[4], final_gain,
                                   final=(layer == depth - 1))
    return tuple(st[0].reshape(st[3], st[4], -1) for st in state)


def kernel(x_prompt, x_sample, mem_prompt, mem_sample, ffn1_norm, ffn1_w_gate, ffn1_w_up, ffn1_w_down, mix_norm, w_in, attn_sink, ssm_lambda_re, ssm_lambda_im, ssm_log_dt, ssm_b_re, ssm_b_im, ssm_c_re, ssm_c_im, ssm_d, ssm_w_glu, ssm_b_glu, conv_w, conv_b, conv_ln_g, conv_ln_b, attn_out_norm, ssm_out_norm, conv_out_norm, w_out, xattn_norm, mem_norm, xattn_w_q, xattn_w_kv, xattn_w_o, ffn2_norm, ffn2_w_gate, ffn2_w_up, ffn2_w_down, final_norm):
    stacked = {
        'ffn1_norm': ffn1_norm, 'ffn1_w_gate': ffn1_w_gate, 'ffn1_w_up': ffn1_w_up, 'ffn1_w_down': ffn1_w_down,
        'mix_norm': mix_norm, 'w_in': w_in, 'attn_sink': attn_sink,
        'ssm_lambda_re': ssm_lambda_re, 'ssm_lambda_im': ssm_lambda_im, 'ssm_log_dt': ssm_log_dt,
        'ssm_b_re': ssm_b_re, 'ssm_b_im': ssm_b_im, 'ssm_c_re': ssm_c_re, 'ssm_c_im': ssm_c_im,
        'ssm_d': ssm_d, 'ssm_w_glu': ssm_w_glu, 'ssm_b_glu': ssm_b_glu,
        'conv_w': conv_w, 'conv_b': conv_b, 'conv_ln_g': conv_ln_g, 'conv_ln_b': conv_ln_b,
        'attn_out_norm': attn_out_norm, 'ssm_out_norm': ssm_out_norm, 'conv_out_norm': conv_out_norm,
        'w_out': w_out,
        'xattn_norm': xattn_norm, 'mem_norm': mem_norm,
        'xattn_w_q': xattn_w_q, 'xattn_w_kv': xattn_w_kv, 'xattn_w_o': xattn_w_o,
        'ffn2_norm': ffn2_norm, 'ffn2_w_gate': ffn2_w_gate, 'ffn2_w_up': ffn2_w_up, 'ffn2_w_down': ffn2_w_down,
    }
    y_prompt, y_sample = _trunk([(x_prompt, mem_prompt), (x_sample, mem_sample)], stacked, final_norm)
    return (y_prompt, y_sample)
```

```python
import functools
import math

import jax
import jax.numpy as jnp
from jax import lax
from jax.experimental import pallas as pl
from jax.experimental.pallas import tpu as pltpu

F32 = jnp.float32
BF16 = jnp.bfloat16

D_MODEL = 2048
DEPTH = 4
HEAD_DIM = 128
N_Q_HEADS = 8
N_KV_HEADS = 2
Q_PER_KV = N_Q_HEADS // N_KV_HEADS
ATTN_W = N_Q_HEADS * HEAD_DIM
KV_W = N_KV_HEADS * HEAD_DIM
WINDOW = 128
ROPE_DIM = HEAD_DIM // 4
ROPE_THETA = 500000.0
SSM_W = 512
SSM_H = 16
SSM_G = SSM_W // SSM_H
SSM_P = 64
CONV_W = 512
CONV_K = 31
D_MIX = ATTN_W + SSM_W + CONV_W
IN_COLS = ATTN_W + 2 * KV_W + SSM_W + 2 * CONV_W
MEM_LEN = 256
X_HEADS = 4
X_HEAD_DIM = 128
X_W = X_HEADS * X_HEAD_DIM
D_FF = 5632
EPS = 1e-6
NEG = -1e30

O_Q = 0
O_KV = ATTN_W
O_U = ATTN_W + 2 * KV_W
O_C = O_U + SSM_W

SSM_T = 16
SSM_PAIRS = SSM_G // 2
SSM_TH = SSM_T * SSM_H

V7X_VMEM_BYTES = 64 * 1024 * 1024
VMEM_LIMIT = V7X_VMEM_BYTES - 8 * 1024 * 1024
LANES = 128
SUBLANES = 8
CONV_HALO = 2 * SUBLANES


def _cparams(sem):
    return pltpu.CompilerParams(dimension_semantics=sem, vmem_limit_bytes=VMEM_LIMIT)


def _const_spec(shape):
    nd = len(shape)
    return pl.BlockSpec(shape, lambda *_: (0,) * nd)


def _layer_spec(layer, shape, imap=None):
    if imap is None:
        imap = lambda *_: (0,) * len(shape)
    return pl.BlockSpec((None,) + tuple(shape), lambda *idx: (layer,) + tuple(imap(*idx)))


def _rmsnorm(xf, g):
    ms = jnp.mean(xf * xf, axis=-1, keepdims=True)
    return xf * lax.rsqrt(ms + EPS) * g


def _row_tile(n, pref):
    t = min(n, pref)
    assert n % t == 0, (n, t)
    return t


def _ffn_body(x_ref, g_ref, wg_ref, wu_ref, wd_ref, fg_ref, o_ref, h_ref, *, n_ff, final):
    j = pl.program_id(1)

    @pl.when(j == 0)
    def _():
        tm = o_ref.shape[0]
        rc = min(tm, 2 * LANES)
        for r0 in range(0, tm, rc):
            h_ref[r0:r0 + rc, :] = _rmsnorm(x_ref[r0:r0 + rc, :], g_ref[...]).astype(BF16)
        o_ref[...] = jnp.zeros_like(o_ref)

    h = h_ref[...]
    g = jnp.dot(h, wg_ref[...], preferred_element_type=F32)
    u = jnp.dot(h, wu_ref[...], preferred_element_type=F32)
    a = (jax.nn.silu(g) * u).astype(BF16)
    o_ref[...] += jnp.dot(a, wd_ref[...], preferred_element_type=F32)

    @pl.when(j == n_ff - 1)
    def _():
        tm = o_ref.shape[0]
        rc = min(tm, LANES)

        def chunk(r, carry):
            rows = pl.ds(pl.multiple_of(r * rc, rc), rc)
            y = x_ref[rows, :] + 0.5 * o_ref[rows, :]
            if final:
                y = _rmsnorm(y, fg_ref[...])
            o_ref[rows, :] = y
            return carry

        lax.fori_loop(0, tm // rc, chunk, 0)


def _ffn(x, gain, wg, wu, wd, final_gain, layer, *, final, tm=1024, tf=512):
    n, d = x.shape
    dff = wg.shape[-1]
    tm = _row_tile(n, tm)
    tf = _row_tile(dff, tf)
    n_ff = dff // tf
    return pl.pallas_call(
        functools.partial(_ffn_body, n_ff=n_ff, final=final),
        grid=(n // tm, n_ff),
        in_specs=[
            pl.BlockSpec((tm, d), lambda i, j: (i, 0)),
            _const_spec((1, d)),
            _layer_spec(layer, (d, tf), lambda i, j: (0, j)),
            _layer_spec(layer, (d, tf), lambda i, j: (0, j)),
            _layer_spec(layer, (tf, d), lambda i, j: (j, 0)),
            _const_spec((1, d)),
        ],
        out_specs=pl.BlockSpec((tm, d), lambda i, j: (i, 0)),
        out_shape=jax.ShapeDtypeStruct((n, d), F32),
        scratch_shapes=[pltpu.VMEM((tm, d), BF16)],
        compiler_params=_cparams(("parallel", "arbitrary")),
        name="ffn",
    )(x, gain, wg, wu, wd, final_gain)


def _rope_tables(seq_len):
    half = ROPE_DIM // 2
    inv_freq = ROPE_THETA ** (-(2.0 * jnp.arange(half, dtype=F32)) / ROPE_DIM)
    ang = jnp.arange(seq_len, dtype=jnp.int32).astype(F32)[:, None] * inv_freq[None, :]
    cos = jnp.cos(ang)
    sin = jnp.sin(ang)
    zeros = jnp.zeros((seq_len, HEAD_DIM - ROPE_DIM), F32)
    zh = jnp.zeros((seq_len, half), F32)
    cos_t = jnp.concatenate([cos, cos, jnp.ones_like(zeros)], axis=1)
    s1_t = jnp.concatenate([-sin, zh, zeros], axis=1)
    s2_t = jnp.concatenate([zh, sin, zeros], axis=1)
    return cos_t, s1_t, s2_t


def _inproj_body(x_ref, g_ref, w_ref, cos_ref, s1_ref, s2_ref, q_ref, kv_ref, u_ref, c_ref):
    h = _rmsnorm(x_ref[...], g_ref[...]).astype(BF16)
    cos_t, s1_t, s2_t = cos_ref[...], s1_ref[...], s2_ref[...]
    half = ROPE_DIM // 2

    def rope(z):
        return (z * cos_t + pltpu.roll(z, HEAD_DIM - half, 1) * s1_t
                + pltpu.roll(z, half, 1) * s2_t)

    zq = jnp.dot(h, w_ref[:, O_Q:O_KV], preferred_element_type=F32)
    for hd in range(N_Q_HEADS):
        sl = slice(hd * HEAD_DIM, (hd + 1) * HEAD_DIM)
        q_ref[:, sl] = rope(zq[:, sl]).astype(BF16)
    zkv = jnp.dot(h, w_ref[:, O_KV:O_U], preferred_element_type=F32)
    for hd in range(N_KV_HEADS):
        sl = slice(hd * HEAD_DIM, (hd + 1) * HEAD_DIM)
        kv_ref[:, sl] = rope(zkv[:, sl]).astype(BF16)
    kv_ref[:, KV_W:] = zkv[:, KV_W:].astype(BF16)
    u_ref[...] = jnp.dot(h, w_ref[:, O_U:O_C], preferred_element_type=F32)
    c_ref[...] = jnp.dot(h, w_ref[:, O_C:], preferred_element_type=F32)


def _inproj(x, gain, w_in, layer, rope_tabs, seq_len, *, tm=512):
    n, d = x.shape
    tm = _row_tile(seq_len, tm)
    per_seq = seq_len // tm
    tab_spec = pl.BlockSpec((tm, HEAD_DIM), lambda i: (i % per_seq, 0))
    row = lambda w: pl.BlockSpec((tm, w), lambda i: (i, 0))
    return pl.pallas_call(
        _inproj_body,
        grid=(n // tm,),
        in_specs=[row(d), _const_spec((1, d)), _layer_spec(layer, (d, IN_COLS)), tab_spec, tab_spec, tab_spec],
        out_specs=[row(ATTN_W), row(2 * KV_W), row(SSM_W), row(2 * CONV_W)],
        out_shape=[
            jax.ShapeDtypeStruct((n, ATTN_W), BF16),
            jax.ShapeDtypeStruct((n, 2 * KV_W), BF16),
            jax.ShapeDtypeStruct((n, SSM_W), F32),
            jax.ShapeDtypeStruct((n, 2 * CONV_W), F32),
        ],
        compiler_params=_cparams(("parallel",)),
        name="inproj",
    )(x, gain, w_in, *rope_tabs)


def _attn_body(sink_ref, q_ref, kvc_ref, kvp_ref, kvn_ref, gn_ref, o_ref, kvbuf, acc_ref,
               *, tq, seq_len):
    i = pl.program_id(1)
    w = WINDOW
    nk = 3 * w
    kvbuf[0:w, :] = kvp_ref[...]
    kvbuf[w:w + tq, :] = kvc_ref[...]
    kvbuf[w + tq:, :] = kvn_ref[...]
    t = lax.broadcasted_iota(jnp.int32, (w, nk), 0)
    c = lax.broadcasted_iota(jnp.int32, (w, nk), 1)
    scale = 1.0 / math.sqrt(HEAD_DIM)
    for sb in range(tq // w):
        p0 = i * tq + sb * w
        lo = jnp.maximum(t, w - p0)
        hi = jnp.minimum(t + 2 * w, seq_len - p0 + w - 1)
        bias = jnp.where(c >= lo, jnp.where(c <= hi, 0.0, NEG), NEG)
        rows = slice(sb * w, (sb + 1) * w)
        krows = slice(sb * w, sb * w + nk)
        for hk in range(N_KV_HEADS):
            k = kvbuf[krows, hk * HEAD_DIM:(hk + 1) * HEAD_DIM]
            v = kvbuf[krows, KV_W + hk * HEAD_DIM:KV_W + (hk + 1) * HEAD_DIM]
            heads = range(hk * Q_PER_KV, (hk + 1) * Q_PER_KV)
            qs = jnp.concatenate([q_ref[rows, h * HEAD_DIM:(h + 1) * HEAD_DIM] for h in heads], axis=0)
            s = lax.dot_general(qs, k, (((1,), (1,)), ((), ())), preferred_element_type=F32) * scale
            ps, dens = [], []
            for g, h in enumerate(heads):
                sg = s[g * w:(g + 1) * w] + bias
                sk = sink_ref[h]
                m = jnp.maximum(jnp.max(sg, axis=-1, keepdims=True), sk)
                p = jnp.exp(sg - m)
                dens.append(jnp.sum(p, axis=-1, keepdims=True) + jnp.exp(sk - m))
                ps.append(p.astype(BF16))
            o = jnp.dot(jnp.concatenate(ps, axis=0), v, preferred_element_type=F32)
            for g, h in enumerate(heads):
                acc_ref[rows, h * HEAD_DIM:(h + 1) * HEAD_DIM] = o[g * w:(g + 1) * w] / dens[g]
    o_ref[...] = _rmsnorm(acc_ref[...], gn_ref[...]).astype(BF16)


def _window_attn(q, kv, sink, gain, batch, seq_len, *, tq=512):
    n = q.shape[0]
    tq = _row_tile(seq_len, tq)
    nq = seq_len // tq
    wpt = tq // WINDOW
    n_wblk = n // WINDOW
    return pl.pallas_call(
        functools.partial(_attn_body, tq=tq, seq_len=seq_len),
        grid=(batch, nq),
        in_specs=[
            pl.BlockSpec(memory_space=pltpu.SMEM),
            pl.BlockSpec((tq, ATTN_W), lambda b, i: (b * nq + i, 0)),
            pl.BlockSpec((tq, 2 * KV_W), lambda b, i: (b * nq + i, 0)),
            pl.BlockSpec((WINDOW, 2 * KV_W),
                         lambda b, i: (jnp.maximum((b * nq + i) * wpt - 1, 0), 0)),
            pl.BlockSpec((WINDOW, 2 * KV_W),
                         lambda b, i: (jnp.minimum((b * nq + i + 1) * wpt, n_wblk - 1), 0)),
            _const_spec((1, ATTN_W)),
        ],
        out_specs=pl.BlockSpec((tq, ATTN_W), lambda b, i: (b * nq + i, 0)),
        out_shape=jax.ShapeDtypeStruct((n, ATTN_W), BF16),
        scratch_shapes=[pltpu.VMEM((tq + 2 * WINDOW, 2 * KV_W), BF16),
                        pltpu.VMEM((tq, ATTN_W), F32)],
        compiler_params=_cparams(("parallel", "parallel")),
        name="window_attn",
    )(sink, q, kv, kv, kv, gain)


def _s5_operators(lam_re, lam_im, log_dt, b_re, b_im, c_re, c_im, nlev):
    hp = lax.Precision.HIGHEST
    T, H, P, G = SSM_T, SSM_H, SSM_P, SSM_G
    lr = jnp.minimum(lam_re.astype(F32), -1e-4)
    li = lam_im.astype(F32)
    dt = jnp.exp(log_dt.astype(F32))[..., None]
    lg = lr * dt
    th = li * dt

    def cpow(tau):
        tau = tau[:, None, None, None]
        mag = jnp.exp(lg * tau)
        return mag * jnp.cos(th * tau), mag * jnp.sin(th * tau)

    ab_re, ab_im = (v[0] for v in cpow(jnp.ones((1,), F32)))
    den = lr * lr + li * li
    n_re = ab_re - 1.0
    n_im = ab_im
    f_re = (n_re * lr + n_im * li) / den
    f_im = (n_im * lr - n_re * li) / den
    br = b_re.astype(F32).transpose(0, 1, 3, 2)
    bi = b_im.astype(F32).transpose(0, 1, 3, 2)
    bb_re = f_re[:, :, None, :] * br - f_im[:, :, None, :] * bi
    bb_im = f_re[:, :, None, :] * bi + f_im[:, :, None, :] * br
    cr = c_re.astype(F32)
    ci = c_im.astype(F32)

    pw_re, pw_im = cpow(jnp.arange(T + 1, dtype=F32))
    w_re = pw_re[:, :, :, None, :] * bb_re - pw_im[:, :, :, None, :] * bb_im
    w_im = pw_re[:, :, :, None, :] * bb_im + pw_im[:, :, :, None, :] * bb_re
    kmat = (jnp.einsum('dgop,tdgip->dgito', cr, w_re, precision=hp)
            - jnp.einsum('dgop,tdgip->dgito', ci, w_im, precision=hp))

    kf = kmat[0][:, :, :T]
    kb = kmat[1][:, :, :T][:, :, ::-1]
    lag = jnp.concatenate([kb[:, :, :T - 1], kb[:, :, T - 1:] + kf[:, :, :1], kf[:, :, 1:],
                           jnp.zeros((G, H, 1, H), F32)], axis=2).reshape(G, H, 2 * T * H)
    m_intra = jnp.stack([lag[:, :, (T - 1 - s) * H:(T - 1 - s) * H + T * H] for s in range(T)], axis=1)
    m_intra = m_intra.reshape(G, T * H, T * H)

    def to_in(wd):
        return wd.transpose(1, 0, 2, 3).reshape(G, T * H, P)

    rev = jnp.arange(T - 1, -1, -1)
    fwd = jnp.arange(T)
    m_in = jnp.stack([to_in(w_re[rev, 0]), to_in(w_im[rev, 0]),
                      to_in(w_re[fwd, 1]), to_in(w_im[fwd, 1])], axis=2)

    def to_out(pr, pi, d):
        wr = cr[d][None] * pr[:, :, None, :] - ci[d][None] * pi[:, :, None, :]
        wi = cr[d][None] * pi[:, :, None, :] + ci[d][None] * pr[:, :, None, :]
        tr = lambda a: a.transpose(1, 0, 2, 3).reshape(G, T * H, P).swapaxes(1, 2)
        return tr(wr), tr(-wi)

    of_re, of_im = to_out(pw_re[1:, 0], pw_im[1:, 0], 0)
    ob_re, ob_im = to_out(pw_re[T - fwd, 1], pw_im[T - fwd, 1], 1)
    m_out = jnp.stack([of_re, of_im, ob_re, ob_im], axis=1)

    eye = jnp.eye(2, dtype=F32)
    J = SSM_PAIRS
    m_in = m_in.reshape(J, 2, T * H, 4, P)
    half = (jnp.arange(2 * P) // P)[None, :] == jnp.arange(2)[:, None]
    w_in = jnp.where(half[None, :, None, None, :], jnp.concatenate([m_in, m_in], axis=-1), 0.0)
    w_in = w_in.reshape(J, 2 * T * H, 8 * P)
    w_u = m_intra.reshape(J, 2, T * H, 1, T * H) * eye[None, :, None, :, None]
    w_u = w_u.reshape(J, 2 * T * H, 2 * T * H)
    w_s = m_out.reshape(J, 2, 4, P, T * H).transpose(0, 2, 1, 3, 4)
    w_s = w_s[:, :, :, :, None, :] * eye[None, None, :, None, :, None]
    w_s = w_s.reshape(J, 8 * P, 2 * T * H)
    w_out = jnp.concatenate([w_u, w_s], axis=1)

    lv_re, lv_im = cpow(jnp.asarray([float(T * 2 ** k) for k in range(nlev)], F32))
    lv = jnp.stack([lv_re, lv_im], axis=2)
    lv = lv.reshape(nlev, 2, 2, J, 2 * P).transpose(3, 1, 2, 0, 4).reshape(J, 4, nlev, 2 * P)
    return w_in.astype(BF16), w_out.astype(BF16), lv


def _s5_body(u_ref, win_ref, wout_ref, lv_ref, y_ref, *, n_chunks, nlev):
    rows = u_ref.shape[0] // SSM_T
    sw = 2 * SSM_P
    gpt = LANES // SSM_H
    ppt = gpt // 2
    gran = lax.broadcasted_iota(jnp.int32, (rows, LANES), 1) // SSM_H
    pos = lax.broadcasted_iota(jnp.int32, (rows, sw), 0) & (n_chunks - 1)

    def granule_transpose(tiles):
        tiles = list(tiles)
        step = gpt // 2
        while step >= 1:
            low = (gran & step) == 0
            nxt = list(tiles)
            for a in range(gpt):
                if a & step == 0:
                    lo, hi = tiles[a], tiles[a + step]
                    nxt[a] = jnp.where(low, lo, pltpu.roll(hi, step * SSM_H, 1))
                    nxt[a + step] = jnp.where(low, pltpu.roll(lo, LANES - step * SSM_H, 1), hi)
            tiles = nxt
            step //= 2
        return tiles

    def shifted(v, shift, fwd):
        if shift % SUBLANES == 0:
            z = jnp.zeros((shift, v.shape[1]), v.dtype)
            moved = (jnp.concatenate([z, v[:rows - shift]], axis=0) if fwd
                     else jnp.concatenate([v[shift:], z], axis=0))
            if rows == n_chunks:
                return moved
        else:
            moved = pltpu.roll(v, shift if fwd else rows - shift, 0)
        keep = (pos >= shift) if fwd else (pos < n_chunks - shift)
        return jnp.where(keep, moved, 0.0)

    src = [u_ref[pl.ds(t, rows, stride=SSM_T), :] for t in range(SSM_T)]
    by_group = [granule_transpose(src[th * gpt:(th + 1) * gpt]) for th in range(2)]
    ytiles = [[None] * gpt for _ in range(2)]
    for pi in range(ppt):
        u32 = jnp.concatenate([by_group[th][2 * pi + e] for e in range(2) for th in range(2)], axis=1)
        u = u32.astype(BF16)
        x = jnp.dot(u, win_ref[pi], preferred_element_type=F32)
        states = []
        for d, fwd in ((0, True), (1, False)):
            s_re = x[:, (2 * d) * sw:(2 * d + 1) * sw]
            s_im = x[:, (2 * d + 1) * sw:(2 * d + 2) * sw]
            for k in range(nlev):
                shift = 2 ** k
                if shift >= n_chunks:
                    break
                a_re = lv_ref[pi, 2 * d, k:k + 1, :]
                a_im = lv_ref[pi, 2 * d + 1, k:k + 1, :]
                p_re = shifted(s_re, shift, fwd)
                p_im = shifted(s_im, shift, fwd)
                s_re, s_im = (s_re + a_re * p_re - a_im * p_im,
                              s_im + a_re * p_im + a_im * p_re)
            states.append(shifted(s_re, 1, fwd))
            states.append(shifted(s_im, 1, fwd))
        lhs = jnp.concatenate([u32] + states, axis=1).astype(BF16)
        y = jnp.dot(lhs, wout_ref[pi], preferred_element_type=F32)
        for e in range(2):
            for th in range(2):
                ytiles[th][2 * pi + e] = y[:, (2 * e + th) * LANES:(2 * e + th + 1) * LANES]
    for th in range(2):
        for t8, tile in enumerate(granule_transpose(ytiles[th])):
            y_ref[pl.ds(th * gpt + t8, rows, stride=SSM_T), :] = tile


def _s5_scan(u, w_in, w_out, lv, layer, seq_len, *, rows=128):
    n = u.shape[0]
    nrow = n // SSM_T
    n_chunks = seq_len // SSM_T
    rows = max(min(rows, nrow), n_chunks)
    assert nrow % rows == 0 and rows % n_chunks == 0
    assert n_chunks & (n_chunks - 1) == 0
    nlev = lv.shape[-2]
    assert 2 ** nlev >= n_chunks
    pw = 2 * SSM_TH
    ppt = LANES // SSM_H // 2
    tile = pl.BlockSpec((rows * SSM_T, LANES), lambda o, r: (r, o))
    return pl.pallas_call(
        functools.partial(_s5_body, n_chunks=n_chunks, nlev=nlev),
        grid=(SSM_PAIRS // ppt, nrow // rows),
        in_specs=[
            tile,
            _layer_spec(layer, (ppt, pw, 8 * SSM_P), lambda o, r: (o, 0, 0)),
            _layer_spec(layer, (ppt, pw + 8 * SSM_P, pw), lambda o, r: (o, 0, 0)),
            _layer_spec(layer, (ppt, 4, nlev, 2 * SSM_P), lambda o, r: (o, 0, 0, 0)),
        ],
        out_specs=tile,
        out_shape=jax.ShapeDtypeStruct((n, SSM_W), F32),
        compiler_params=_cparams(("parallel", "parallel")),
        name="s5_scan",
    )(u, w_in, w_out, lv)


def _s5_post_body(y_ref, u_ref, d_ref, w_ref, b_ref, gn_ref, o_ref):
    y = y_ref[...] + d_ref[...] * u_ref[...]
    g = jax.nn.gelu(y)
    z = jnp.dot(g.astype(BF16), w_ref[...], preferred_element_type=F32) + b_ref[...]
    s = g * jax.nn.sigmoid(z)
    o_ref[...] = _rmsnorm(s, gn_ref[...]).astype(BF16)


def _s5_post(y, u, d_skip, w_glu, layer, b_glu, gain, *, tm=1024):
    n = y.shape[0]
    tm = _row_tile(n, tm)
    row = pl.BlockSpec((tm, SSM_W), lambda i: (i, 0))
    vec = _const_spec((1, SSM_W))
    return pl.pallas_call(
        _s5_post_body,
        grid=(n // tm,),
        in_specs=[row, row, vec, _layer_spec(layer, (SSM_W, SSM_W)), vec, vec],
        out_specs=row,
        out_shape=jax.ShapeDtypeStruct((n, SSM_W), BF16),
        compiler_params=_cparams(("parallel",)),
        name="s5_post",
    )(y, u, d_skip, w_glu, b_glu, gain)


def _conv_body(cc_ref, cp_ref, cn_ref, w_ref, b_ref, lng_ref, lnb_ref, gn_ref, o_ref, gbuf, sbuf, cbuf,
               *, tl, nblk, rb):
    i = pl.program_id(1)

    def glu(v):
        return v[:, :CONV_W] * jax.nn.sigmoid(v[:, CONV_W:])

    gbuf[0:CONV_HALO, :] = jnp.where(i > 0, glu(cp_ref[...]), 0.0)
    gbuf[CONV_HALO:CONV_HALO + tl, :] = glu(cc_ref[...])
    gbuf[CONV_HALO + tl:2 * CONV_HALO + tl, :] = jnp.where(i < nblk - 1, glu(cn_ref[...]), 0.0)
    span = tl + 2 * CONV_HALO - SUBLANES
    for b in range(1, SUBLANES):
        sbuf[b - 1, :, :] = gbuf[b:b + span, :]
    base = CONV_HALO - CONV_K // 2

    def block(blk, carry):
        r0 = blk * rb
        acc = jnp.broadcast_to(b_ref[...], (rb // SUBLANES, SUBLANES, CONV_W))
        for k in range(CONV_K):
            a, b = divmod(base + k, SUBLANES)
            rows = pl.ds(pl.multiple_of(r0 + a * SUBLANES, SUBLANES), rb)
            src = gbuf[rows, :] if b == 0 else sbuf[b - 1, rows, :]
            acc = acc + src.reshape(rb // SUBLANES, SUBLANES, CONV_W) * w_ref[k]
        cbuf[pl.ds(pl.multiple_of(r0, rb), rb), :] = acc.reshape(rb, CONV_W)
        return carry

    lax.fori_loop(0, tl // rb, block, 0)

    nr = min(tl, LANES)
    for r0 in range(0, tl, nr):
        acc = cbuf[r0:r0 + nr, :]
        mu = jnp.mean(acc, axis=-1, keepdims=True)
        xc = acc - mu
        var = jnp.mean(xc * xc, axis=-1, keepdims=True)
        y = xc * lax.rsqrt(var + EPS) * lng_ref[...] + lnb_ref[...]
        y = jax.nn.silu(y)
        o_ref[r0:r0 + nr, :] = _rmsnorm(y, gn_ref[...]).astype(BF16)


def _conformer_conv(c, conv_w, conv_b, ln_g, ln_b, gain, batch, seq_len, *, tl=512, rb=32):
    n = c.shape[0]
    tl = _row_tile(seq_len, tl)
    rb = _row_tile(tl, rb)
    nblk = seq_len // tl
    hpt = tl // CONV_HALO
    n_hblk = n // CONV_HALO
    vec = _const_spec((1, CONV_W))
    return pl.pallas_call(
        functools.partial(_conv_body, tl=tl, nblk=nblk, rb=rb),
        grid=(batch, nblk),
        in_specs=[
            pl.BlockSpec((tl, 2 * CONV_W), lambda b, i: (b * nblk + i, 0)),
            pl.BlockSpec((CONV_HALO, 2 * CONV_W),
                         lambda b, i: (jnp.maximum((b * nblk + i) * hpt - 1, 0), 0)),
            pl.BlockSpec((CONV_HALO, 2 * CONV_W),
                         lambda b, i: (jnp.minimum((b * nblk + i + 1) * hpt, n_hblk - 1), 0)),
            _const_spec((CONV_K, SUBLANES, CONV_W)),
            vec, vec, vec, vec,
        ],
        out_specs=pl.BlockSpec((tl, CONV_W), lambda b, i: (b * nblk + i, 0)),
        out_shape=jax.ShapeDtypeStruct((n, CONV_W), BF16),
        scratch_shapes=[pltpu.VMEM((tl + 2 * CONV_HALO, CONV_W), F32),
                        pltpu.VMEM((SUBLANES - 1, tl + 2 * CONV_HALO - SUBLANES, CONV_W), F32),
                        pltpu.VMEM((tl, CONV_W), F32)],
        compiler_params=_cparams(("parallel", "parallel")),
        name="conformer_conv",
    )(c, c, c, jnp.broadcast_to(conv_w[:, None, :], (CONV_K, SUBLANES, CONV_W)), conv_b, ln_g, ln_b, gain)


def _memkv_body(m_ref, g_ref, w_ref, o_ref):
    h = _rmsnorm(m_ref[...], g_ref[...]).astype(BF16)
    o_ref[...] = jnp.dot(h, w_ref[...], preferred_element_type=F32).astype(BF16)


def _mem_kv(mem, gain, w_kv, layer, *, tm=512):
    n, d = mem.shape
    tm = _row_tile(n, tm)
    return pl.pallas_call(
        _memkv_body,
        grid=(n // tm,),
        in_specs=[pl.BlockSpec((tm, d), lambda i: (i, 0)), _const_spec((1, d)),
                  _layer_spec(layer, (d, 2 * X_W))],
        out_specs=pl.BlockSpec((tm, 2 * X_W), lambda i: (i, 0)),
        out_shape=jax.ShapeDtypeStruct((n, 2 * X_W), BF16),
        compiler_params=_cparams(("parallel",)),
        name="mem_kv",
    )(mem, gain, w_kv)


def _mixout_body(x_ref, a_ref, s_ref, c_ref, wout_ref, gx_ref, wq_ref, kv_ref, wo_ref, o_ref):
    mixed = jnp.concatenate([a_ref[...], s_ref[...], c_ref[...]], axis=1)
    x1 = x_ref[...] + jnp.dot(mixed, wout_ref[...], preferred_element_type=F32)
    h = _rmsnorm(x1, gx_ref[...]).astype(BF16)
    q = jnp.dot(h, wq_ref[...], preferred_element_type=F32).astype(BF16)
    scale = 1.0 / math.sqrt(X_HEAD_DIM)
    heads = []
    for hd in range(X_HEADS):
        ks = slice(hd * X_HEAD_DIM, (hd + 1) * X_HEAD_DIM)
        vs = slice(X_W + hd * X_HEAD_DIM, X_W + (hd + 1) * X_HEAD_DIM)
        s = lax.dot_general(q[:, ks], kv_ref[:, ks], (((1,), (1,)), ((), ())),
                            preferred_element_type=F32) * scale
        m = jnp.max(s, axis=-1, keepdims=True)
        p = jnp.exp(s - m)
        denom = jnp.sum(p, axis=-1, keepdims=True)
        o = jnp.dot(p.astype(BF16), kv_ref[:, vs], preferred_element_type=F32) / denom
        heads.append(o.astype(BF16))
    o = jnp.concatenate(heads, axis=1)
    o_ref[...] = x1 + jnp.dot(o, wo_ref[...], preferred_element_type=F32)


def _mix_out(x, a, s, c, w_out, gain_x, w_q, kv_mem, w_o, layer, batch, seq_len, *, tm=512):
    n, d = x.shape
    tm = _row_tile(seq_len, tm)
    per_seq = seq_len // tm
    row = lambda w: pl.BlockSpec((tm, w), lambda b, i: (b * per_seq + i, 0))
    return pl.pallas_call(
        _mixout_body,
        grid=(batch, per_seq),
        in_specs=[
            row(d), row(ATTN_W), row(SSM_W), row(CONV_W),
            _layer_spec(layer, (D_MIX, d)), _const_spec((1, d)), _layer_spec(layer, (d, X_W)),
            pl.BlockSpec((MEM_LEN, 2 * X_W), lambda b, i: (b, 0)),
            _layer_spec(layer, (X_W, d)),
        ],
        out_specs=row(d),
        out_shape=jax.ShapeDtypeStruct((n, d), F32),
        compiler_params=_cparams(("parallel", "parallel")),
        name="mix_out",
    )(x, a, s, c, w_out, gain_x, w_q, kv_mem, w_o)


def _encoder_layer(x, mem, p, w, layer, rope_tabs, batch, seq_len, final_gain, final):
    vec = lambda a: a.reshape(1, -1).astype(F32)
    x = _ffn(x, vec(p['ffn1_norm']), w['ffn1_w_gate'], w['ffn1_w_up'], w['ffn1_w_down'],
             final_gain, layer, final=False)
    q, kv, u, c = _inproj(x, vec(p['mix_norm']), w['w_in'], layer, rope_tabs, seq_len)
    a_out = _window_attn(q, kv, p['attn_sink'].astype(F32), vec(p['attn_out_norm']), batch, seq_len)
    y = _s5_scan(u, w['s5_w_in'], w['s5_w_out'], w['s5_lv'], layer, seq_len)
    s_out = _s5_post(y, u, vec(p['ssm_d']), w['ssm_w_glu'], layer, vec(p['ssm_b_glu']),
                     vec(p['ssm_out_norm']))
    c_out = _conformer_conv(c, p['conv_w'].astype(F32), vec(p['conv_b']), vec(p['conv_ln_g']),
                            vec(p['conv_ln_b']), vec(p['conv_out_norm']), batch, seq_len)
    kv_mem = _mem_kv(mem, vec(p['mem_norm']), w['xattn_w_kv'], layer)
    x = _mix_out(x, a_out, s_out, c_out, w['w_out'], vec(p['xattn_norm']), w['xattn_w_q'], kv_mem,
                 w['xattn_w_o'], layer, batch, seq_len)
    x = _ffn(x, vec(p['ffn2_norm']), w['ffn2_w_gate'], w['ffn2_w_up'], w['ffn2_w_down'],
             final_gain, layer, final=final)
    return x


_MATMUL_WEIGHTS = ('ffn1_w_gate', 'ffn1_w_up', 'ffn1_w_down', 'w_in', 'ssm_w_glu', 'w_out',
                   'xattn_w_q', 'xattn_w_kv', 'xattn_w_o', 'ffn2_w_gate', 'ffn2_w_up', 'ffn2_w_down')


def _trunk(groups, stacked, final_norm):
    depth = stacked['w_in'].shape[0]
    final_gain = final_norm.reshape(1, -1).astype(F32)
    max_len = max(x.shape[1] for x, _ in groups)
    nlev = max(1, (max_len // SSM_T - 1).bit_length())
    state = []
    for x, mem in groups:
        b, l, d = x.shape
        state.append([x.reshape(b * l, d), mem.reshape(-1, d), _rope_tables(l), b, l])
    w = {k: stacked[k].astype(BF16) for k in _MATMUL_WEIGHTS}
    w['s5_w_in'], w['s5_w_out'], w['s5_lv'] = jax.vmap(functools.partial(_s5_operators, nlev=nlev))(
        stacked['ssm_lambda_re'], stacked['ssm_lambda_im'], stacked['ssm_log_dt'], stacked['ssm_b_re'],
        stacked['ssm_b_im'], stacked['ssm_c_re'], stacked['ssm_c_im'])
    small = [k for k in stacked if k not in _MATMUL_WEIGHTS]
    for layer in range(depth):
        p = {k: stacked[k][layer] for k in small}
        for st in state:
            st[0] = _encoder_layer(st[0], st[1], p, w, layer, st[2], st[3], st[4], final_gain,
                                   final=(layer == depth - 1))
    return tuple(st[0].reshape(st[3], st[4], -1) for st in state)


def kernel(x_prompt, x_sample, mem_prompt, mem_sample, ffn1_norm, ffn1_w_gate, ffn1_w_up, ffn1_w_down, mix_norm, w_in, attn_sink, ssm_lambda_re, ssm_lambda_im, ssm_log_dt, ssm_b_re, ssm_b_im, ssm_c_re, ssm_c_im, ssm_d, ssm_w_glu, ssm_b_glu, conv_w, conv_b, conv_ln_g, conv_ln_b, attn_out_norm, ssm_out_norm, conv_out_norm, w_out, xattn_norm, mem_norm, xattn_w_q, xattn_w_kv, xattn_w_o, ffn2_norm, ffn2_w_gate, ffn2_w_up, ffn2_w_down, final_norm):
    stacked = {
        'ffn1_norm': ffn1_norm, 'ffn1_w_gate': ffn1_w_gate, 'ffn1_w_up': ffn1_w_up, 'ffn1_w_down': ffn1_w_down,
        'mix_norm': mix_norm, 'w_in': w_in, 'attn_sink': attn_sink,
        'ssm_lambda_re': ssm_lambda_re, 'ssm_lambda_im': ssm_lambda_im, 'ssm_log_dt': ssm_log_dt,
        'ssm_b_re': ssm_b_re, 'ssm_b_im': ssm_b_im, 'ssm_c_re': ssm_c_re, 'ssm_c_im': ssm_c_im,
        'ssm_d': ssm_d, 'ssm_w_glu': ssm_w_glu, 'ssm_b_glu': ssm_b_glu,
        'conv_w': conv_w, 'conv_b': conv_b, 'conv_ln_g': conv_ln_g, 'conv_ln_b': conv_ln_b,
        'attn_out_norm': attn_out_norm, 'ssm_out_norm': ssm_out_norm, 'conv_out_norm': conv_out_norm,
        'w_out': w_out,
        'xattn_norm': xattn_norm, 'mem_norm': mem_norm,
        'xattn_w_q': xattn_w_q, 'xattn_w_kv': xattn_w_kv, 'xattn_w_o': xattn_w_o,
        'ffn2_norm': ffn2_norm, 'ffn2_w_gate': ffn2_w_gate, 'ffn2_w_up': ffn2_w_up, 'ffn2_w_down': ffn2_w_down,
    }
    y_prompt, y_sample = _trunk([(x_prompt, mem_prompt), (x_sample, mem_sample)], stacked, final_norm)
    return (y_prompt, y_sample)
```
